```python
import math
import jax, jax.numpy as jnp
from jax import lax
import numpy as np

D_MODEL = 2048
BATCH = 2
SEQ = 16384
DEPTH = 1

HEAD_DIM = 128
GRID_W = 64
Q_BLOCK = 128
EPS = 1e-6
A_HEADS = D_MODEL // (2 * HEAD_DIM)
A_KV_HEADS = A_HEADS // 4
A_GROUP = A_HEADS // A_KV_HEADS
A_ROPE_THETA = 10000.0
AXIAL_DIM = HEAD_DIM // 2
B_VDIM = 2 * HEAD_DIM
B_HEADS = D_MODEL // (2 * B_VDIM)
PARTIAL_ROPE_DIM = HEAD_DIM // 4
PARTIAL_ROPE_THETA = 500000.0
A_Q = A_HEADS * HEAD_DIM
A_KV = A_KV_HEADS * HEAD_DIM
B_QK = B_HEADS * 2 * HEAD_DIM
B_V = B_HEADS * B_VDIM
D_IN = A_Q + 2 * A_KV + 2 * B_QK + B_V
D_MIX = A_Q + B_V
D_FF = 5632

kernel_name = "hybrid_gqa_axial_diffattn_macaron"


def _rms_norm(x, gain):
    x32 = x.astype(jnp.float32)
    y = x32 * lax.rsqrt(jnp.mean(x32 * x32, axis=-1, keepdims=True) + EPS)
    return (y * gain.astype(jnp.float32)).astype(x.dtype)


def _rope_table(pos, dim, theta):
    inv_freq = theta ** (-jnp.arange(0, dim, 2, dtype=jnp.float32) / dim)
    ang = pos[:, None] * inv_freq[None, :]
    return jnp.cos(ang), jnp.sin(ang)


def _apply_rope(x, cos, sin):
    half = x.shape[-1] // 2
    shape = (1, x.shape[1]) + (1,) * (x.ndim - 3) + (half,)
    c = cos.reshape(shape).astype(x.dtype)
    s = sin.reshape(shape).astype(x.dtype)
    x1, x2 = x[..., :half], x[..., half:]
    return jnp.concatenate([x1 * c - x2 * s, x2 * c + x1 * s], axis=-1)


def _axial_rope(x, row_cs, col_cs):
    return jnp.concatenate([
        _apply_rope(x[..., :AXIAL_DIM], *row_cs),
        _apply_rope(x[..., AXIAL_DIM:], *col_cs)], axis=-1)


def _partial_rope(x, cs):
    return jnp.concatenate([
        _apply_rope(x[..., :PARTIAL_ROPE_DIM], *cs),
        x[..., PARTIAL_ROPE_DIM:]], axis=-1)


def _sweep_queries(fn, q):
    b, s = q.shape[0], q.shape[1]
    nb = s // Q_BLOCK
    qb = jnp.moveaxis(q.reshape((b, nb, Q_BLOCK) + q.shape[2:]), 1, 0)
    out = lax.map(fn, qb)
    return jnp.moveaxis(out, 0, 1).reshape(b, s, -1)


def _swiglu(h, w_gu, w_down):
    g, u = jnp.split(h @ w_gu, 2, axis=-1)
    return (jax.nn.silu(g) * u) @ w_down


def _gqa_axial(qa, ka, va):
    scale = HEAD_DIM ** -0.5
    b = qa.shape[0]

    def block(qb):
        qb = qb.reshape(b, Q_BLOCK, A_KV_HEADS, A_GROUP, HEAD_DIM)
        s = jnp.einsum('bqgrd,bkgd->bgrqk', qb, ka).astype(jnp.float32) * scale
        p = jax.nn.softmax(s, axis=-1).astype(va.dtype)
        o = jnp.einsum('bgrqk,bkgd->bqgrd', p, va)
        return o.reshape(b, Q_BLOCK, A_Q)

    return _sweep_queries(block, qa)


def _diff_attention(qd, kd, vd, lam, lambda_init, subln):
    scale = HEAD_DIM ** -0.5
    b = qd.shape[0]

    def block(qb):
        s = jnp.einsum('bqhcd,bkhcd->bhcqk', qb, kd).astype(jnp.float32) * scale
        p = jax.nn.softmax(s, axis=-1)
        attn = (p[:, :, 0] - lam * p[:, :, 1]).astype(vd.dtype)
        o = jnp.einsum('bhqk,bkhe->bqhe', attn, vd)
        o = _rms_norm(o, subln) * (1.0 - lambda_init)
        return o.reshape(b, Q_BLOCK, B_V)

    return _sweep_queries(block, qd)


def setup_inputs(seed: int = 0) -> dict:
    key = jax.random.key(seed)
    ks = jax.random.split(key, 24)
    f32 = jnp.float32

    def w(k, shape, fan_in):
        return jax.random.normal(k, shape, f32) * (fan_in ** -0.5)

    def gain(k, shape):
        return 1.0 + 0.02 * jax.random.normal(k, shape, f32)

    return {
        "x": jax.random.normal(ks[0], (BATCH, SEQ, D_MODEL), f32),
        "ffn1_norm": gain(ks[1], (DEPTH, D_MODEL)),
        "ffn1_w_gu": w(ks[2], (DEPTH, D_MODEL, 2 * D_FF), D_MODEL),
        "ffn1_w_down": w(ks[3], (DEPTH, D_FF, D_MODEL), D_FF),
        "mix_norm": gain(ks[4], (DEPTH, D_MODEL)),
        "w_in": w(ks[5], (DEPTH, D_MODEL, D_IN), D_MODEL),
        "a_q_norm": gain(ks[6], (DEPTH, HEAD_DIM)),
        "a_k_norm": gain(ks[7], (DEPTH, HEAD_DIM)),
        "b_q_norm": gain(ks[8], (DEPTH, HEAD_DIM)),
        "b_k_norm": gain(ks[9], (DEPTH, HEAD_DIM)),
        "b_lambda_q1": 0.1 * jax.random.normal(ks[10], (DEPTH, HEAD_DIM), f32),
        "b_lambda_k1": 0.1 * jax.random.normal(ks[11], (DEPTH, HEAD_DIM), f32),
        "b_lambda_q2": 0.1 * jax.random.normal(ks[12], (DEPTH, HEAD_DIM), f32),
        "b_lambda_k2": 0.1 * jax.random.normal(ks[13], (DEPTH, HEAD_DIM), f32),
        "b_subln": gain(ks[14], (DEPTH, B_VDIM)),
        "w_out": w(ks[15], (DEPTH, D_MIX, D_MODEL), D_MIX),
        "ffn2_norm": gain(ks[16], (DEPTH, D_MODEL)),
        "ffn2_w_gu": w(ks[17], (DEPTH, D_MODEL, 2 * D_FF), D_MODEL),
        "ffn2_w_down": w(ks[18], (DEPTH, D_FF, D_MODEL), D_FF),
        "out_norm": gain(ks[19], (DEPTH, D_MODEL)),
    }


def reference(x, ffn1_norm, ffn1_w_gu, ffn1_w_down, mix_norm, w_in,
              a_q_norm, a_k_norm, b_q_norm, b_k_norm,
              b_lambda_q1, b_lambda_k1, b_lambda_q2, b_lambda_k2, b_subln,
              w_out, ffn2_norm, ffn2_w_gu, ffn2_w_down, out_norm):
    b, s, _ = x.shape
    rows = s // GRID_W
    row_pos = jnp.broadcast_to(jnp.arange(rows, dtype=jnp.float32)[:, None], (rows, GRID_W)).reshape(-1)
    col_pos = jnp.broadcast_to(jnp.arange(GRID_W, dtype=jnp.float32)[None, :], (rows, GRID_W)).reshape(-1)
    lin_pos = jnp.arange(s, dtype=jnp.float32)
    row_cs = _rope_table(row_pos, AXIAL_DIM, A_ROPE_THETA)
    col_cs = _rope_table(col_pos, AXIAL_DIM, A_ROPE_THETA)
    part_cs = _rope_table(lin_pos, PARTIAL_ROPE_DIM, PARTIAL_ROPE_THETA)

    for l in range(DEPTH):
        x = x + 0.5 * _swiglu(_rms_norm(x, ffn1_norm[l]), ffn1_w_gu[l], ffn1_w_down[l])

        h = _rms_norm(x, mix_norm[l])
        proj = h @ w_in[l]
        i0 = A_Q; i1 = i0 + A_KV; i2 = i1 + A_KV; i3 = i2 + B_QK; i4 = i3 + B_QK
        qa = proj[..., :i0].reshape(b, s, A_HEADS, HEAD_DIM)
        ka = proj[..., i0:i1].reshape(b, s, A_KV_HEADS, HEAD_DIM)
        va = proj[..., i1:i2].reshape(b, s, A_KV_HEADS, HEAD_DIM)
        qd = proj[..., i2:i3].reshape(b, s, B_HEADS, 2, HEAD_DIM)
        kd = proj[..., i3:i4].reshape(b, s, B_HEADS, 2, HEAD_DIM)
        vd = proj[..., i4:].reshape(b, s, B_HEADS, B_VDIM)

        qa = _axial_rope(_rms_norm(qa, a_q_norm[l]), row_cs, col_cs)
        ka = _axial_rope(_rms_norm(ka, a_k_norm[l]), row_cs, col_cs)
        out_a = _gqa_axial(qa, ka, va)

        qd = _partial_rope(_rms_norm(qd, b_q_norm[l]), part_cs)
        kd = _partial_rope(_rms_norm(kd, b_k_norm[l]), part_cs)
        lambda_init = 0.8 - 0.6 * math.exp(-0.3 * l)
        lam = (jnp.exp(jnp.sum(b_lambda_q1[l].astype(jnp.float32) * b_lambda_k1[l].astype(jnp.float32)))
               - jnp.exp(jnp.sum(b_lambda_q2[l].astype(jnp.float32) * b_lambda_k2[l].astype(jnp.float32)))
               + lambda_init)
        out_b = _diff_attention(qd, kd, vd, lam, lambda_init, b_subln[l])

        x = x + jnp.concatenate([out_a, out_b], axis=-1) @ w_out[l]

        x = x + 0.5 * _swiglu(_rms_norm(x, ffn2_norm[l]), ffn2_w_gu[l], ffn2_w_down[l])

        x = _rms_norm(x, out_norm[l])
    return x
```

```python
import functools
import math

import jax
import jax.numpy as jnp
from jax import lax
from jax.experimental import pallas as pl
from jax.experimental.pallas import tpu as pltpu

HEAD_DIM = 128
GRID_W = 64
EPS = 1e-6
A_ROPE_THETA = 10000.0
PARTIAL_ROPE_THETA = 500000.0
AXIAL_DIM = HEAD_DIM // 2
PARTIAL_ROPE_DIM = HEAD_DIM // 4
A_GROUP = 4
LANES = 128
LOG2E = math.log2(math.e)
VMEM_LIMIT_BYTES = 56 * 1024 * 1024
NEG_BIG = -1e30

BF16 = jnp.bfloat16
F32 = jnp.float32


def _dot(a, b):
    return jnp.dot(a, b, preferred_element_type=F32)


def _rms(x, gain):
    r = lax.rsqrt(jnp.mean(x * x, axis=-1, keepdims=True) + EPS)
    return (x * r) * gain


def _params(*semantics):
    return pltpu.CompilerParams(dimension_semantics=semantics, vmem_limit_bytes=VMEM_LIMIT_BYTES)


def _ffn_kernel(x_ref, gain_ref, wg_ref, wu_ref, wd_ref, ngain_ref, o_ref, *rest, emit_norm, final_norm):
    if emit_norm:
        hn_ref, xn_ref = rest
    else:
        (xn_ref,) = rest
    j = pl.program_id(1)

    @pl.when(j == 0)
    def _():
        x = x_ref[...]
        xn_ref[...] = _rms(x, gain_ref[...]).astype(BF16)
        o_ref[...] = x

    xn = xn_ref[...]
    g = _dot(xn, wg_ref[...])
    u = _dot(xn, wu_ref[...])
    act = (0.5 * g / (1.0 + jnp.exp(-g))) * u
    o_ref[...] += _dot(act.astype(BF16), wd_ref[...])

    @pl.when(j == pl.num_programs(1) - 1)
    def _():
        y = _rms(o_ref[...], ngain_ref[...])
        if emit_norm:
            hn_ref[...] = y.astype(BF16)
        if final_norm:
            o_ref[...] = y


def _ffn(x, gain, w_gu, w_down, next_gain, *, emit_norm, final_norm, tm, tf):
    m, d = x.shape
    d_ff = w_down.shape[0]
    nf = d_ff // tf
    assert m % tm == 0 and d_ff % tf == 0
    out_shape = [jax.ShapeDtypeStruct((m, d), F32)]
    out_specs = [pl.BlockSpec((tm, d), lambda i, j: (i, 0))]
    if emit_norm:
        out_shape.append(jax.ShapeDtypeStruct((m, d), BF16))
        out_specs.append(pl.BlockSpec((tm, d), lambda i, j: (i, 0)))
    res = pl.pallas_call(
        functools.partial(_ffn_kernel, emit_norm=emit_norm, final_norm=final_norm),
        grid=(m // tm, nf),
        in_specs=[
            pl.BlockSpec((tm, d), lambda i, j: (i, 0)),
            pl.BlockSpec((1, d), lambda i, j: (0, 0)),
            pl.BlockSpec((d, tf), lambda i, j: (0, j)),
            pl.BlockSpec((d, tf), lambda i, j: (0, j + nf)),
            pl.BlockSpec((tf, d), lambda i, j: (j, 0)),
            pl.BlockSpec((1, d), lambda i, j: (0, 0)),
        ],
        out_specs=out_specs,
        out_shape=out_shape,
        scratch_shapes=[pltpu.VMEM((tm, d), BF16)],
        compiler_params=_params("parallel", "arbitrary"),
        name="ffn",
    )(x, gain, w_gu, w_gu, w_down, next_gain)
    return res


def _proj_kernel(*refs, rope, shift, n_q_chunks, q_scale):
    if rope:
        xn_ref, w_ref, gain_ref, cos_ref, sp_ref, sm_ref, o_ref = refs
    else:
        xn_ref, w_ref, o_ref = refs
    acc = _dot(xn_ref[...], w_ref[...])
    if not rope:
        o_ref[...] = acc.astype(BF16)
        return
    cos, sp, sm = cos_ref[...], sp_ref[...], sm_ref[...]
    for c in range(acc.shape[1] // LANES):
        sl = slice(c * LANES, (c + 1) * LANES)
        y = _rms(acc[:, sl], gain_ref[:, sl])
        y = y * cos + pltpu.roll(y, shift, 1) * sp + pltpu.roll(y, LANES - shift, 1) * sm
        if c < n_q_chunks:
            y = y * q_scale
        o_ref[:, sl] = y.astype(BF16)


def _proj(xn, w_in, col0, width, *, tm, seq, rope_args=None):
    m, d = xn.shape
    assert m % tm == 0 and seq % tm == 0 and col0 % width == 0
    cb = col0 // width
    in_specs = [
        pl.BlockSpec((tm, d), lambda i: (i, 0)),
        pl.BlockSpec((d, width), lambda i: (0, cb)),
    ]
    args = [xn, w_in]
    kw = dict(rope=False, shift=0, n_q_chunks=0, q_scale=1.0)
    if rope_args is not None:
        gain, cos, sp, sm, shift, n_q_chunks, q_scale = rope_args
        nt = seq // tm
        in_specs += [pl.BlockSpec((1, width), lambda i: (0, 0))]
        in_specs += [pl.BlockSpec((tm, LANES), lambda i: (i % nt, 0))] * 3
        args += [gain, cos, sp, sm]
        kw = dict(rope=True, shift=shift, n_q_chunks=n_q_chunks, q_scale=q_scale)
    return pl.pallas_call(
        functools.partial(_proj_kernel, **kw),
        grid=(m // tm,),
        in_specs=in_specs,
        out_specs=pl.BlockSpec((tm, width), lambda i: (i, 0)),
        out_shape=jax.ShapeDtypeStruct((m, width), BF16),
        compiler_params=_params("parallel"),
        name="proj",
    )(*args)


def _attn_kernel(*refs, n_q, ksrc, diff, lambda_init):
    if diff:
        q_ref, k_ref, v_ref, lq1_ref, lk1_ref, lq2_ref, lk2_ref, subln_ref, o_ref, acc_ref, m_ref, l_ref = refs
    else:
        q_ref, k_ref, v_ref, o_ref, acc_ref, m_ref, l_ref = refs
    n_chunks = k_ref.shape[1]

    acc_ref[...] = jnp.zeros(acc_ref.shape, F32)
    m_ref[...] = jnp.full(m_ref.shape, NEG_BIG, F32)
    l_ref[...] = jnp.zeros(l_ref.shape, F32)

    def body(c, carry):
        vt = v_ref[c]
        for r in range(n_q):
            s = _dot(k_ref[ksrc[r], c], q_ref[r])
            m_old = m_ref[r]
            m_new = jnp.maximum(m_old, jnp.max(s, axis=0, keepdims=True))
            alpha = jnp.exp2(m_old - m_new)
            p = jnp.exp2(s - m_new)
            l_ref[r] = alpha * l_ref[r] + jnp.sum(p, axis=0, keepdims=True)
            acc_ref[r] = alpha * acc_ref[r] + _dot(vt, p.astype(BF16))
            m_ref[r] = m_new
        return carry

    lax.fori_loop(0, n_chunks, body, 0)

    if not diff:
        for r in range(n_q):
            o = acc_ref[r] / l_ref[r]
            o_ref[:, r * LANES:(r + 1) * LANES] = o.T.astype(BF16)
    else:
        lam = (jnp.exp(jnp.sum(lq1_ref[...] * lk1_ref[...], axis=-1, keepdims=True))
               - jnp.exp(jnp.sum(lq2_ref[...] * lk2_ref[...], axis=-1, keepdims=True))
               + lambda_init)
        o = acc_ref[0] / l_ref[0] - lam * (acc_ref[1] / l_ref[1])
        y = _rms(o.T, subln_ref[...]) * (1.0 - lambda_init)
        o_ref[...] = y.astype(BF16)


def _attn(qt, k, vt, *, tq, ksrc, diff_args=None, lambda_init=0.0):
    b, g, n_q, hd, s = qt.shape
    n_k, nc, tk = k.shape[2], k.shape[3], k.shape[4]
    dv = vt.shape[3]
    diff = diff_args is not None
    width = dv if diff else n_q * dv
    in_specs = [
        pl.BlockSpec((None, None, n_q, hd, tq), lambda bi, gi, qi: (bi, gi, 0, 0, qi)),
        pl.BlockSpec((None, None, n_k, nc, tk, hd), lambda bi, gi, qi: (bi, gi, 0, 0, 0, 0)),
        pl.BlockSpec((None, None, nc, dv, tk), lambda bi, gi, qi: (bi, gi, 0, 0, 0)),
    ]
    args = [qt, k, vt]
    if diff:
        in_specs += [pl.BlockSpec((1, hd), lambda bi, gi, qi: (0, 0))] * 4
        in_specs += [pl.BlockSpec((1, dv), lambda bi, gi, qi: (0, 0))]
        args += list(diff_args)
    return pl.pallas_call(
        functools.partial(_attn_kernel, n_q=n_q, ksrc=ksrc, diff=diff, lambda_init=lambda_init),
        grid=(b, g, s // tq),
        in_specs=in_specs,
        out_specs=pl.BlockSpec((None, tq, width), lambda bi, gi, qi: (bi, qi, gi)),
        out_shape=jax.ShapeDtypeStruct((b, s, g * width), BF16),
        scratch_shapes=[
            pltpu.VMEM((n_q, dv, tq), F32),
            pltpu.VMEM((n_q, 1, tq), F32),
            pltpu.VMEM((n_q, 1, tq), F32),
        ],
        compiler_params=_params("parallel", "parallel", "arbitrary"),
        name="attn_diff" if diff else "attn_gqa",
    )(*args)


def _outproj_kernel(x_ref, a_ref, b_ref, wa_ref, wb_ref, o_ref):
    o_ref[...] = x_ref[...] + _dot(a_ref[...], wa_ref[...]) + _dot(b_ref[...], wb_ref[...])


def _outproj(x, a, b, w_out, *, tm):
    m, d = x.shape
    da, db = a.shape[1], b.shape[1]
    assert da == db and w_out.shape[0] == da + db
    return pl.pallas_call(
        _outproj_kernel,
        grid=(m // tm,),
        in_specs=[
            pl.BlockSpec((tm, d), lambda i: (i, 0)),
            pl.BlockSpec((tm, da), lambda i: (i, 0)),
            pl.BlockSpec((tm, db), lambda i: (i, 0)),
            pl.BlockSpec((da, d), lambda i: (0, 0)),
            pl.BlockSpec((db, d), lambda i: (1, 0)),
        ],
        out_specs=pl.BlockSpec((tm, d), lambda i: (i, 0)),
        out_shape=jax.ShapeDtypeStruct((m, d), F32),
        compiler_params=_params("parallel"),
        name="outproj",
    )(x, a, b, w_out, w_out)


def _rope_cs(pos, dim, theta):
    inv_freq = theta ** (-jnp.arange(0, dim, 2, dtype=F32) / dim)
    ang = pos[:, None] * inv_freq[None, :]
    return jnp.cos(ang), jnp.sin(ang)


def _rope_tables(seq):
    rows = seq // GRID_W
    row_pos = jnp.broadcast_to(jnp.arange(rows, dtype=F32)[:, None], (rows, GRID_W)).reshape(-1)
    col_pos = jnp.broadcast_to(jnp.arange(GRID_W, dtype=F32)[None, :], (rows, GRID_W)).reshape(-1)
    lin_pos = jnp.arange(seq, dtype=F32)
    rc, rs = _rope_cs(row_pos, AXIAL_DIM, A_ROPE_THETA)
    cc, cs = _rope_cs(col_pos, AXIAL_DIM, A_ROPE_THETA)
    pc, ps = _rope_cs(lin_pos, PARTIAL_ROPE_DIM, PARTIAL_ROPE_THETA)
    z = jnp.zeros_like(rs)
    cos_a = jnp.concatenate([rc, rc, cc, cc], axis=-1)
    sm_a = jnp.concatenate([-rs, z, -cs, z], axis=-1)
    sp_a = jnp.concatenate([z, rs, z, cs], axis=-1)
    rest = HEAD_DIM - PARTIAL_ROPE_DIM
    zp = jnp.zeros_like(ps)
    cos_b = jnp.concatenate([pc, pc, jnp.ones((seq, rest), F32)], axis=-1)
    sm_b = jnp.concatenate([-ps, zp, jnp.zeros((seq, rest), F32)], axis=-1)
    sp_b = jnp.concatenate([zp, ps, jnp.zeros((seq, rest), F32)], axis=-1)
    return (cos_a, sp_a, sm_a), (cos_b, sp_b, sm_b)


def _tiles(m, seq, d_ff):
    def fit(n, t):
        t = min(t, n)
        while n % t:
            t //= 2
        return t
    return dict(
        ffn_tm=fit(m, 512), ffn_tf=fit(d_ff, 512),
        proj_tm=fit(seq, 512), out_tm=fit(m, 512),
        a_tq=fit(seq, 256), b_tq=fit(seq, 512), tk=fit(seq, 512),
    )


def kernel(x, ffn1_norm, ffn1_w_gu, ffn1_w_down, mix_norm, w_in, a_q_norm, a_k_norm, b_q_norm, b_k_norm,
           b_lambda_q1, b_lambda_k1, b_lambda_q2, b_lambda_k2, b_subln, w_out, ffn2_norm, ffn2_w_gu,
           ffn2_w_down, out_norm):
    bsz, seq, d = x.shape
    depth = w_in.shape[0]
    d_ff = ffn1_w_down.shape[1]
    m = bsz * seq
    hd = HEAD_DIM
    a_heads = d // (2 * hd)
    a_kv = a_heads // A_GROUP
    b_vdim = 2 * hd
    b_heads = d // (2 * b_vdim)
    a_q, a_kvw = a_heads * hd, a_kv * hd
    b_qk, b_v = b_heads * 2 * hd, b_heads * b_vdim
    t = _tiles(m, seq, d_ff)
    tk = t["tk"]
    nc = seq // tk
    q_scale = (hd ** -0.5) * LOG2E
    tab_a, tab_b = _rope_tables(seq)

    x = x.reshape(m, d)
    for l in range(depth):
        lambda_init = 0.8 - 0.6 * math.exp(-0.3 * l)
        row = lambda v: v[l].reshape(1, -1).astype(F32)
        w_gu1, w_d1 = ffn1_w_gu[l].astype(BF16), ffn1_w_down[l].astype(BF16)
        w_gu2, w_d2 = ffn2_w_gu[l].astype(BF16), ffn2_w_down[l].astype(BF16)
        w_i, w_o = w_in[l].astype(BF16), w_out[l].astype(BF16)

        x, h = _ffn(x, row(ffn1_norm), w_gu1, w_d1, row(mix_norm),
                    emit_norm=True, final_norm=False, tm=t["ffn_tm"], tf=t["ffn_tf"])

        gain_a = jnp.concatenate([jnp.tile(row(a_q_norm), (1, a_heads)), jnp.tile(row(a_k_norm), (1, a_kv))], axis=-1)
        wa = a_q + a_kvw
        qk_a = _proj(h, w_i, 0, wa, tm=t["proj_tm"], seq=seq,
                     rope_args=(gain_a, *tab_a, AXIAL_DIM // 2, a_heads, q_scale))
        v_a = _proj_cols(h, w_i, wa, a_kvw, tm=t["proj_tm"], seq=seq)
        gain_b = jnp.concatenate([jnp.tile(row(b_q_norm), (1, 2 * b_heads)), jnp.tile(row(b_k_norm), (1, 2 * b_heads))], axis=-1)
        c0 = a_q + 2 * a_kvw
        qk_b = _proj_cols(h, w_i, c0, 2 * b_qk, tm=t["proj_tm"], seq=seq,
                          rope_args=(gain_b, *tab_b, PARTIAL_ROPE_DIM // 2, 2 * b_heads, q_scale))
        v_b = _proj_cols(h, w_i, c0 + 2 * b_qk, b_v, tm=t["proj_tm"], seq=seq)

        qa_t = qk_a[:, :a_q].reshape(bsz, seq, a_kv, A_GROUP, hd).transpose(0, 2, 3, 4, 1)
        ka = qk_a[:, a_q:].reshape(bsz, nc, tk, a_kv, 1, hd).transpose(0, 3, 4, 1, 2, 5)
        va_t = v_a.reshape(bsz, nc, tk, a_kv, hd).transpose(0, 3, 1, 4, 2)
        qb_t = qk_b[:, :b_qk].reshape(bsz, seq, b_heads, 2, hd).transpose(0, 2, 3, 4, 1)
        kb = qk_b[:, b_qk:].reshape(bsz, nc, tk, b_heads, 2, hd).transpose(0, 3, 4, 1, 2, 5)
        vb_t = v_b.reshape(bsz, nc, tk, b_heads, b_vdim).transpose(0, 3, 1, 4, 2)

        out_a = _attn(qa_t, ka, va_t, tq=t["a_tq"], ksrc=(0,) * A_GROUP)
        out_b = _attn(qb_t, kb, vb_t, tq=t["b_tq"], ksrc=(0, 1),
                      diff_args=(row(b_lambda_q1), row(b_lambda_k1), row(b_lambda_q2), row(b_lambda_k2), row(b_subln)),
                      lambda_init=lambda_init)

        x = _outproj(x, out_a.reshape(m, a_q), out_b.reshape(m, b_v), w_o, tm=t["out_tm"])
        x, = _ffn(x, row(ffn2_norm), w_gu2, w_d2, row(out_norm),
                  emit_norm=False, final_norm=True, tm=t["ffn_tm"], tf=t["ffn_tf"])
    return x.reshape(bsz, seq, d)


def _proj_cols(xn, w_in, col0, width, *, tm, seq, rope_args=None):
    if col0 % width:
        return _proj(xn, lax.slice_in_dim(w_in, col0, col0 + width, axis=1), 0, width, tm=tm, seq=seq, rope_args=rope_args)
    return _proj(xn, w_in, col0, width, tm=tm, seq=seq, rope_args=rope_args)
```

```python
import functools
import math

import jax
import jax.numpy as jnp
from jax import lax
from jax.experimental import pallas as pl
from jax.experimental.pallas import tpu as pltpu

HEAD_DIM = 128
GRID_W = 64
EPS = 1e-6
A_ROPE_THETA = 10000.0
PARTIAL_ROPE_THETA = 500000.0
AXIAL_DIM = HEAD_DIM // 2
PARTIAL_ROPE_DIM = HEAD_DIM // 4
A_GROUP = 4
LANES = 128
LOG2E = math.log2(math.e)
VMEM_LIMIT_BYTES = 56 * 1024 * 1024
NEG_BIG = -1e30
L_MIN_SAFE = 1e-20

BF16 = jnp.bfloat16
F32 = jnp.float32


def _dot(a, b):
    return jnp.dot(a, b, preferred_element_type=F32)


def _rms(x, gain):
    r = lax.rsqrt(jnp.mean(x * x, axis=-1, keepdims=True) + EPS)
    return (x * r) * gain


def _params(*semantics):
    return pltpu.CompilerParams(dimension_semantics=semantics, vmem_limit_bytes=VMEM_LIMIT_BYTES)


def _ffn_kernel(x_ref, gain_ref, wg_ref, wu_ref, wd_ref, ngain_ref, o_ref, *rest, emit_norm, final_norm):
    if emit_norm:
        hn_ref, xn_ref = rest
    else:
        (xn_ref,) = rest
    j = pl.program_id(1)

    @pl.when(j == 0)
    def _():
        x = x_ref[...]
        xn_ref[...] = _rms(x, gain_ref[...]).astype(BF16)
        o_ref[...] = x

    xn = xn_ref[...]
    g = _dot(xn, wg_ref[...])
    u = _dot(xn, wu_ref[...])
    act = (0.5 * g / (1.0 + jnp.exp(-g))) * u
    o_ref[...] += _dot(act.astype(BF16), wd_ref[...])

    @pl.when(j == pl.num_programs(1) - 1)
    def _():
        y = _rms(o_ref[...], ngain_ref[...])
        if emit_norm:
            hn_ref[...] = y.astype(BF16)
        if final_norm:
            o_ref[...] = y


def _ffn(x, gain, w_gu, w_down, next_gain, *, emit_norm, final_norm, tm, tf):
    m, d = x.shape
    d_ff = w_down.shape[0]
    nf = d_ff // tf
    assert m % tm == 0 and d_ff % tf == 0
    out_shape = [jax.ShapeDtypeStruct((m, d), F32)]
    out_specs = [pl.BlockSpec((tm, d), lambda i, j: (i, 0))]
    if emit_norm:
        out_shape.append(jax.ShapeDtypeStruct((m, d), BF16))
        out_specs.append(pl.BlockSpec((tm, d), lambda i, j: (i, 0)))
    res = pl.pallas_call(
        functools.partial(_ffn_kernel, emit_norm=emit_norm, final_norm=final_norm),
        grid=(m // tm, nf),
        in_specs=[
            pl.BlockSpec((tm, d), lambda i, j: (i, 0)),
            pl.BlockSpec((1, d), lambda i, j: (0, 0)),
            pl.BlockSpec((d, tf), lambda i, j: (0, j)),
            pl.BlockSpec((d, tf), lambda i, j: (0, j + nf)),
            pl.BlockSpec((tf, d), lambda i, j: (j, 0)),
            pl.BlockSpec((1, d), lambda i, j: (0, 0)),
        ],
        out_specs=out_specs,
        out_shape=out_shape,
        scratch_shapes=[pltpu.VMEM((tm, d), BF16)],
        compiler_params=_params("parallel", "arbitrary"),
        name="ffn",
    )(x, gain, w_gu, w_gu, w_down, next_gain)
    return res


def _proj_kernel(*refs, rope, shift, n_q_chunks, q_scale):
    if rope:
        xn_ref, w_ref, gain_ref, cos_ref, sp_ref, sm_ref, o_ref = refs
    else:
        xn_ref, w_ref, o_ref = refs
    acc = _dot(xn_ref[...], w_ref[...])
    if not rope:
        o_ref[...] = acc.astype(BF16)
        return
    cos, sp, sm = cos_ref[...], sp_ref[...], sm_ref[...]
    for c in range(acc.shape[1] // LANES):
        sl = slice(c * LANES, (c + 1) * LANES)
        y = _rms(acc[:, sl], gain_ref[:, sl])
        y = y * cos + pltpu.roll(y, shift, 1) * sp + pltpu.roll(y, LANES - shift, 1) * sm
        if c < n_q_chunks:
            y = y * q_scale
        o_ref[:, sl] = y.astype(BF16)


def _proj(xn, w_in, col0, width, *, tm, seq, rope_args=None):
    m, d = xn.shape
    assert m % tm == 0 and seq % tm == 0 and col0 % width == 0
    cb = col0 // width
    in_specs = [
        pl.BlockSpec((tm, d), lambda i: (i, 0)),
        pl.BlockSpec((d, width), lambda i: (0, cb)),
    ]
    args = [xn, w_in]
    kw = dict(rope=False, shift=0, n_q_chunks=0, q_scale=1.0)
    if rope_args is not None:
        gain, cos, sp, sm, shift, n_q_chunks, q_scale = rope_args
        nt = seq // tm
        in_specs += [pl.BlockSpec((1, width), lambda i: (0, 0))]
        in_specs += [pl.BlockSpec((tm, LANES), lambda i: (i % nt, 0))] * 3
        args += [gain, cos, sp, sm]
        kw = dict(rope=True, shift=shift, n_q_chunks=n_q_chunks, q_scale=q_scale)
    return pl.pallas_call(
        functools.partial(_proj_kernel, **kw),
        grid=(m // tm,),
        in_specs=in_specs,
        out_specs=pl.BlockSpec((tm, width), lambda i: (i, 0)),
        out_shape=jax.ShapeDtypeStruct((m, width), BF16),
        compiler_params=_params("parallel"),
        name="proj",
    )(*args)


def _attn_kernel(*refs, tq, diff, lambda_init):
    if diff:
        q_ref, k_ref, v_ref, lq1_ref, lk1_ref, lq2_ref, lk2_ref, subln_ref, o_ref, acc_ref, m_ref, l_ref, l8_ref, kmax_ref = refs
    else:
        q_ref, k_ref, v_ref, o_ref, acc_ref, m_ref, l_ref, l8_ref, kmax_ref = refs
    n_k, n_chunks, tk = k_ref.shape[0], k_ref.shape[1], k_ref.shape[2]
    w = q_ref.shape[1]
    wk = w // n_k

    def scores(c):
        parts = [_dot(k_ref[ks, c], q_ref[:, ks * wk:(ks + 1) * wk]) for ks in range(n_k)]
        return parts[0] if n_k == 1 else jnp.concatenate(parts, axis=1)

    @pl.when(pl.program_id(2) == 0)
    def _():
        for ks in range(n_k):
            def kbody(c, best):
                kf = k_ref[ks, c].astype(F32)
                return jnp.maximum(best, jnp.max(jnp.sum(kf * kf, axis=-1, keepdims=True), axis=0, keepdims=True))
            k2 = lax.fori_loop(0, n_chunks, kbody, jnp.zeros((1, 1), F32))
            kmax_ref[ks] = jnp.broadcast_to(jnp.sqrt(k2), (1, LANES))

    qf = q_ref[...].astype(F32)
    qn = jnp.sqrt(jnp.sum(qf * qf, axis=0, keepdims=True))
    for ks in range(n_k):
        m_ref[:, ks * wk:(ks + 1) * wk] = qn[:, ks * wk:(ks + 1) * wk] * kmax_ref[ks, :, 0:1]
    acc_ref[...] = jnp.zeros(acc_ref.shape, F32)
    l8_ref[...] = jnp.zeros(l8_ref.shape, F32)

    def fast_body(c, carry):
        p = jnp.exp2(scores(c) - m_ref[...])
        l8_ref[...] += jnp.sum(p.reshape(tk // 8, 8, w), axis=0)
        acc_ref[...] += _dot(v_ref[c], p.astype(BF16))
        return carry

    lax.fori_loop(0, n_chunks, fast_body, 0)
    l_ref[...] = jnp.sum(l8_ref[...], axis=0, keepdims=True)
    underflow = jnp.min(l_ref[...]) < L_MIN_SAFE

    @pl.when(underflow)
    def _():
        acc_ref[...] = jnp.zeros(acc_ref.shape, F32)
        m_ref[...] = jnp.full(m_ref.shape, NEG_BIG, F32)
        l_ref[...] = jnp.zeros(l_ref.shape, F32)

        def body(c, carry):
            s = scores(c)
            m_old = m_ref[...]
            m_new = jnp.maximum(m_old, jnp.max(s, axis=0, keepdims=True))
            alpha = jnp.exp2(m_old - m_new)
            p = jnp.exp2(s - m_new)
            l_ref[...] = alpha * l_ref[...] + jnp.sum(p, axis=0, keepdims=True)
            acc_ref[...] = alpha * acc_ref[...] + _dot(v_ref[c], p.astype(BF16))
            m_ref[...] = m_new
            return carry

        lax.fori_loop(0, n_chunks, body, 0)

    o = acc_ref[...] / l_ref[...]
    if not diff:
        for r in range(w // tq):
            o_ref[:, r * LANES:(r + 1) * LANES] = o[:, r * tq:(r + 1) * tq].T.astype(BF16)
    else:
        lam = (jnp.exp(jnp.sum(lq1_ref[...] * lk1_ref[...], axis=-1, keepdims=True))
               - jnp.exp(jnp.sum(lq2_ref[...] * lk2_ref[...], axis=-1, keepdims=True))
               + lambda_init)
        d = o[:, :tq] - lam * o[:, tq:]
        y = _rms(d.T, subln_ref[...]) * (1.0 - lambda_init)
        o_ref[...] = y.astype(BF16)


def _attn(qt, k, vt, *, tq, diff_args=None, lambda_init=0.0):
    b, g, nqt, hd, w = qt.shape
    n_k, nc, tk = k.shape[2], k.shape[3], k.shape[4]
    dv = vt.shape[3]
    diff = diff_args is not None
    width = dv if diff else (w // tq) * dv
    in_specs = [
        pl.BlockSpec((None, None, None, hd, w), lambda bi, gi, qi: (bi, gi, qi, 0, 0)),
        pl.BlockSpec((None, None, n_k, nc, tk, hd), lambda bi, gi, qi: (bi, gi, 0, 0, 0, 0)),
        pl.BlockSpec((None, None, nc, dv, tk), lambda bi, gi, qi: (bi, gi, 0, 0, 0)),
    ]
    args = [qt, k, vt]
    if diff:
        in_specs += [pl.BlockSpec((1, hd), lambda bi, gi, qi: (0, 0))] * 4
        in_specs += [pl.BlockSpec((1, dv), lambda bi, gi, qi: (0, 0))]
        args += list(diff_args)
    return pl.pallas_call(
        functools.partial(_attn_kernel, tq=tq, diff=diff, lambda_init=lambda_init),
        grid=(b, g, nqt),
        in_specs=in_specs,
        out_specs=pl.BlockSpec((None, tq, width), lambda bi, gi, qi: (bi, qi, gi)),
        out_shape=jax.ShapeDtypeStruct((b, nqt * tq, g * width), BF16),
        scratch_shapes=[
            pltpu.VMEM((dv, w), F32),
            pltpu.VMEM((1, w), F32),
            pltpu.VMEM((1, w), F32),
            pltpu.VMEM((8, w), F32),
            pltpu.VMEM((n_k, 1, LANES), F32),
        ],
        compiler_params=_params("parallel", "parallel", "arbitrary"),
        name="attn_diff" if diff else "attn_gqa",
    )(*args)


def _outproj_kernel(x_ref, a_ref, b_ref, wa_ref, wb_ref, o_ref):
    o_ref[...] = x_ref[...] + _dot(a_ref[...], wa_ref[...]) + _dot(b_ref[...], wb_ref[...])


def _outproj(x, a, b, w_out, *, tm):
    m, d = x.shape
    da, db = a.shape[1], b.shape[1]
    assert da == db and w_out.shape[0] == da + db
    return pl.pallas_call(
        _outproj_kernel,
        grid=(m // tm,),
        in_specs=[
            pl.BlockSpec((tm, d), lambda i: (i, 0)),
            pl.BlockSpec((tm, da), lambda i: (i, 0)),
            pl.BlockSpec((tm, db), lambda i: (i, 0)),
            pl.BlockSpec((da, d), lambda i: (0, 0)),
            pl.BlockSpec((db, d), lambda i: (1, 0)),
        ],
        out_specs=pl.BlockSpec((tm, d), lambda i: (i, 0)),
        out_shape=jax.ShapeDtypeStruct((m, d), F32),
        compiler_params=_params("parallel"),
        name="outproj",
    )(x, a, b, w_out, w_out)


def _rope_cs(pos, dim, theta):
    inv_freq = theta ** (-jnp.arange(0, dim, 2, dtype=F32) / dim)
    ang = pos[:, None] * inv_freq[None, :]
    return jnp.cos(ang), jnp.sin(ang)


def _rope_tables(seq):
    rows = seq // GRID_W
    row_pos = jnp.broadcast_to(jnp.arange(rows, dtype=F32)[:, None], (rows, GRID_W)).reshape(-1)
    col_pos = jnp.broadcast_to(jnp.arange(GRID_W, dtype=F32)[None, :], (rows, GRID_W)).reshape(-1)
    lin_pos = jnp.arange(seq, dtype=F32)
    rc, rs = _rope_cs(row_pos, AXIAL_DIM, A_ROPE_THETA)
    cc, cs = _rope_cs(col_pos, AXIAL_DIM, A_ROPE_THETA)
    pc, ps = _rope_cs(lin_pos, PARTIAL_ROPE_DIM, PARTIAL_ROPE_THETA)
    z = jnp.zeros_like(rs)
    cos_a = jnp.concatenate([rc, rc, cc, cc], axis=-1)
    sm_a = jnp.concatenate([-rs, z, -cs, z], axis=-1)
    sp_a = jnp.concatenate([z, rs, z, cs], axis=-1)
    rest = HEAD_DIM - PARTIAL_ROPE_DIM
    zp = jnp.zeros_like(ps)
    cos_b = jnp.concatenate([pc, pc, jnp.ones((seq, rest), F32)], axis=-1)
    sm_b = jnp.concatenate([-ps, zp, jnp.zeros((seq, rest), F32)], axis=-1)
    sp_b = jnp.concatenate([zp, ps, jnp.zeros((seq, rest), F32)], axis=-1)
    return (cos_a, sp_a, sm_a), (cos_b, sp_b, sm_b)


def _tiles(m, seq, d_ff):
    def fit(n, t):
        t = min(t, n)
        while n % t:
            t //= 2
        return t
    return dict(
        ffn_tm=fit(m, 512), ffn_tf=fit(d_ff, 512),
        proj_tm=fit(seq, 512), out_tm=fit(m, 512),
        a_tq=fit(seq, 256), b_tq=fit(seq, 512), tk=fit(seq, 512),
    )


def kernel(x, ffn1_norm, ffn1_w_gu, ffn1_w_down, mix_norm, w_in, a_q_norm, a_k_norm, b_q_norm, b_k_norm,
           b_lambda_q1, b_lambda_k1, b_lambda_q2, b_lambda_k2, b_subln, w_out, ffn2_norm, ffn2_w_gu,
           ffn2_w_down, out_norm):
    bsz, seq, d = x.shape
    depth = w_in.shape[0]
    d_ff = ffn1_w_down.shape[1]
    m = bsz * seq
    hd = HEAD_DIM
    a_heads = d // (2 * hd)
    a_kv = a_heads // A_GROUP
    b_vdim = 2 * hd
    b_heads = d // (2 * b_vdim)
    a_q, a_kvw = a_heads * hd, a_kv * hd
    b_qk, b_v = b_heads * 2 * hd, b_heads * b_vdim
    t = _tiles(m, seq, d_ff)
    tk = t["tk"]
    nc = seq // tk
    q_scale = (hd ** -0.5) * LOG2E
    tab_a, tab_b = _rope_tables(seq)

    x = x.reshape(m, d)
    for l in range(depth):
        lambda_init = 0.8 - 0.6 * math.exp(-0.3 * l)
        row = lambda v: v[l].reshape(1, -1).astype(F32)
        w_gu1, w_d1 = ffn1_w_gu[l].astype(BF16), ffn1_w_down[l].astype(BF16)
        w_gu2, w_d2 = ffn2_w_gu[l].astype(BF16), ffn2_w_down[l].astype(BF16)
        w_i, w_o = w_in[l].astype(BF16), w_out[l].astype(BF16)

        x, h = _ffn(x, row(ffn1_norm), w_gu1, w_d1, row(mix_norm),
                    emit_norm=True, final_norm=False, tm=t["ffn_tm"], tf=t["ffn_tf"])

        gain_a = jnp.concatenate([jnp.tile(row(a_q_norm), (1, a_heads)), jnp.tile(row(a_k_norm), (1, a_kv))], axis=-1)
        wa = a_q + a_kvw
        qk_a = _proj(h, w_i, 0, wa, tm=t["proj_tm"], seq=seq,
                     rope_args=(gain_a, *tab_a, AXIAL_DIM // 2, a_heads, q_scale))
        v_a = _proj_cols(h, w_i, wa, a_kvw, tm=t["proj_tm"], seq=seq)
        gain_b = jnp.concatenate([jnp.tile(row(b_q_norm), (1, 2 * b_heads)), jnp.tile(row(b_k_norm), (1, 2 * b_heads))], axis=-1)
        c0 = a_q + 2 * a_kvw
        qk_b = _proj_cols(h, w_i, c0, 2 * b_qk, tm=t["proj_tm"], seq=seq,
                          rope_args=(gain_b, *tab_b, PARTIAL_ROPE_DIM // 2, 2 * b_heads, q_scale))
        v_b = _proj_cols(h, w_i, c0 + 2 * b_qk, b_v, tm=t["proj_tm"], seq=seq)

        tqa, tqb = t["a_tq"], t["b_tq"]
        qa_t = (qk_a[:, :a_q].reshape(bsz, seq // tqa, tqa, a_kv, A_GROUP, hd).transpose(0, 3, 1, 5, 4, 2)
                .reshape(bsz, a_kv, seq // tqa, hd, A_GROUP * tqa))
        ka = qk_a[:, a_q:].reshape(bsz, nc, tk, a_kv, 1, hd).transpose(0, 3, 4, 1, 2, 5)
        va_t = v_a.reshape(bsz, nc, tk, a_kv, hd).transpose(0, 3, 1, 4, 2)
        qb_t = (qk_b[:, :b_qk].reshape(bsz, seq // tqb, tqb, b_heads, 2, hd).transpose(0, 3, 1, 5, 4, 2)
                .reshape(bsz, b_heads, seq // tqb, hd, 2 * tqb))
        kb = qk_b[:, b_qk:].reshape(bsz, nc, tk, b_heads, 2, hd).transpose(0, 3, 4, 1, 2, 5)
        vb_t = v_b.reshape(bsz, nc, tk, b_heads, b_vdim).transpose(0, 3, 1, 4, 2)

        out_a = _attn(qa_t, ka, va_t, tq=tqa)
        out_b = _attn(qb_t, kb, vb_t, tq=tqb,
                      diff_args=(row(b_lambda_q1), row(b_lambda_k1), row(b_lambda_q2), row(b_lambda_k2), row(b_subln)),
                      lambda_init=lambda_init)

        x = _outproj(x, out_a.reshape(m, a_q), out_b.reshape(m, b_v), w_o, tm=t["out_tm"])
        x, = _ffn(x, row(ffn2_norm), w_gu2, w_d2, row(out_norm),
                  emit_norm=False, final_norm=True, tm=t["ffn_tm"], tf=t["ffn_tf"])
    return x.reshape(bsz, seq, d)


def _proj_cols(xn, w_in, col0, width, *, tm, seq, rope_args=None):
    if col0 % width:
        return _proj(xn, lax.slice_in_dim(w_in, col0, col0 + width, axis=1), 0, width, tm=tm, seq=seq, rope_args=rope_args)
    return _proj(xn, w_in, col0, width, tm=tm, seq=seq, rope_args=rope_args)
```

```python
import functools
import math

import jax
import jax.numpy as jnp
from jax import lax
from jax.experimental import pallas as pl
from jax.experimental.pallas import tpu as pltpu

HEAD_DIM = 128
GRID_W = 64
EPS = 1e-6
A_ROPE_THETA = 10000.0
PARTIAL_ROPE_THETA = 500000.0
AXIAL_DIM = HEAD_DIM // 2
PARTIAL_ROPE_DIM = HEAD_DIM // 4
A_GROUP = 4
LANES = 128
LOG2E = math.log2(math.e)
VMEM_LIMIT_BYTES = 56 * 1024 * 1024
NEG_BIG = -1e30
L_MIN_SAFE = 1e-20

BF16 = jnp.bfloat16
F32 = jnp.float32


def _dot(a, b):
    return jnp.dot(a, b, preferred_element_type=F32)


def _rms(x, gain):
    r = lax.rsqrt(jnp.mean(x * x, axis=-1, keepdims=True) + EPS)
    return (x * r) * gain


def _params(*semantics):
    return pltpu.CompilerParams(dimension_semantics=semantics, vmem_limit_bytes=VMEM_LIMIT_BYTES)


def _ffn_kernel(x_ref, gain_ref, wg_ref, wu_ref, wd_ref, ngain_ref, o_ref, *rest, emit_norm, final_norm):
    if emit_norm:
        hn_ref, xn_ref = rest
    else:
        (xn_ref,) = rest
    j = pl.program_id(1)

    @pl.when(j == 0)
    def _():
        x = x_ref[...]
        xn_ref[...] = _rms(x, gain_ref[...]).astype(BF16)
        o_ref[...] = x

    xn = xn_ref[...]
    g = _dot(xn, wg_ref[...])
    u = _dot(xn, wu_ref[...])
    act = (0.5 * g / (1.0 + jnp.exp(-g))) * u
    o_ref[...] += _dot(act.astype(BF16), wd_ref[...])

    @pl.when(j == pl.num_programs(1) - 1)
    def _():
        y = _rms(o_ref[...], ngain_ref[...])
        if emit_norm:
            hn_ref[...] = y.astype(BF16)
        if final_norm:
            o_ref[...] = y


def _ffn(x, gain, w_gu, w_down, next_gain, *, emit_norm, final_norm, tm, tf):
    m, d = x.shape
    d_ff = w_down.shape[0]
    nf = d_ff // tf
    assert m % tm == 0 and d_ff % tf == 0
    out_shape = [jax.ShapeDtypeStruct((m, d), F32)]
    out_specs = [pl.BlockSpec((tm, d), lambda i, j: (i, 0))]
    if emit_norm:
        out_shape.append(jax.ShapeDtypeStruct((m, d), BF16))
        out_specs.append(pl.BlockSpec((tm, d), lambda i, j: (i, 0)))
    res = pl.pallas_call(
        functools.partial(_ffn_kernel, emit_norm=emit_norm, final_norm=final_norm),
        grid=(m // tm, nf),
        in_specs=[
            pl.BlockSpec((tm, d), lambda i, j: (i, 0)),
            pl.BlockSpec((1, d), lambda i, j: (0, 0)),
            pl.BlockSpec((d, tf), lambda i, j: (0, j)),
            pl.BlockSpec((d, tf), lambda i, j: (0, j + nf)),
            pl.BlockSpec((tf, d), lambda i, j: (j, 0)),
            pl.BlockSpec((1, d), lambda i, j: (0, 0)),
        ],
        out_specs=out_specs,
        out_shape=out_shape,
        scratch_shapes=[pltpu.VMEM((tm, d), BF16)],
        compiler_params=_params("parallel", "arbitrary"),
        name="ffn",
    )(x, gain, w_gu, w_gu, w_down, next_gain)
    return res


PROJ_COLS = 2 * LANES


def _proj_qk_kernel(h_ref, w_ref, gain_ref, cos_ref, sin_ref, q_ref, k_ref, *, n_qc, per_group, tq):
    tm = h_ref.shape[0]
    cos, sin = cos_ref[...], sin_ref[...]
    width = w_ref.shape[1]
    for c0 in range(0, width, PROJ_COLS):
        cols = min(PROJ_COLS, width - c0)
        acc = _dot(h_ref[...], w_ref[:, c0:c0 + cols])
        for ci in range(cols // LANES):
            c = c0 // LANES + ci
            y = _rms(acc[:, ci * LANES:(ci + 1) * LANES], gain_ref[:, c * LANES:(c + 1) * LANES])
            y = y * cos + pltpu.roll(y, LANES // 2, 1) * sin
            if c < n_qc:
                g, r = divmod(c, per_group)
                yt = y.T.astype(BF16)
                for j in range(tm // tq):
                    q_ref[g, j, :, r * tq:(r + 1) * tq] = yt[:, j * tq:(j + 1) * tq]
            else:
                k_ref[c - n_qc] = y.astype(BF16)


def _proj_qk(h, w, gain, cos, sin, *, bsz, seq, n_qc, per_group, tq, tm):
    m, d = h.shape
    width = w.shape[1]
    n_kc = width // LANES - n_qc
    groups = n_qc // per_group
    nt = seq // tm
    assert tm % tq == 0 and seq % tm == 0 and width % LANES == 0
    return pl.pallas_call(
        functools.partial(_proj_qk_kernel, n_qc=n_qc, per_group=per_group, tq=tq),
        grid=(m // tm,),
        in_specs=[
            pl.BlockSpec((tm, d), lambda i: (i, 0)),
            pl.BlockSpec((d, width), lambda i: (0, 0)),
            pl.BlockSpec((1, width), lambda i: (0, 0)),
            pl.BlockSpec((tm, LANES), lambda i: (i % nt, 0)),
            pl.BlockSpec((tm, LANES), lambda i: (i % nt, 0)),
        ],
        out_specs=[
            pl.BlockSpec((None, groups, tm // tq, LANES, per_group * tq), lambda i: (i // nt, 0, i % nt, 0, 0)),
            pl.BlockSpec((None, n_kc, tm, LANES), lambda i: (i // nt, 0, i % nt, 0)),
        ],
        out_shape=[
            jax.ShapeDtypeStruct((bsz, groups, seq // tq, LANES, per_group * tq), BF16),
            jax.ShapeDtypeStruct((bsz, n_kc, seq, LANES), BF16),
        ],
        compiler_params=_params("parallel"),
        name="proj_qk",
    )(h, w, gain, cos, sin)


def _proj_v_kernel(h_ref, w_ref, v_ref):
    dv, width = v_ref.shape[1], w_ref.shape[1]
    step = max(PROJ_COLS, dv)
    for c0 in range(0, width, step):
        cols = min(step, width - c0)
        acc = _dot(h_ref[...], w_ref[:, c0:c0 + cols])
        for ci in range(cols // dv):
            v_ref[c0 // dv + ci] = acc[:, ci * dv:(ci + 1) * dv].T.astype(BF16)


def _proj_v(h, w, *, bsz, seq, dv, tk, tm):
    m, d = h.shape
    width = w.shape[1]
    groups = width // dv
    nt, per_chunk = seq // tm, tk // tm
    assert tk % tm == 0 and seq % tk == 0
    return pl.pallas_call(
        _proj_v_kernel,
        grid=(m // tm,),
        in_specs=[
            pl.BlockSpec((tm, d), lambda i: (i, 0)),
            pl.BlockSpec((d, width), lambda i: (0, 0)),
        ],
        out_specs=pl.BlockSpec((None, groups, None, dv, tm),
                               lambda i: (i // nt, 0, (i % nt) // per_chunk, 0, (i % nt) % per_chunk)),
        out_shape=jax.ShapeDtypeStruct((bsz, groups, seq // tk, dv, tk), BF16),
        compiler_params=_params("parallel"),
        name="proj_v",
    )(h, w)


def _attn_kernel(*refs, tq, diff, lambda_init):
    if diff:
        q_ref, k_ref, v_ref, lq1_ref, lk1_ref, lq2_ref, lk2_ref, subln_ref, o_ref, acc_ref, m_ref, l_ref, l8_ref, kmax_ref = refs
    else:
        q_ref, k_ref, v_ref, o_ref, acc_ref, m_ref, l_ref, l8_ref, kmax_ref = refs
    n_k, n_chunks, tk = k_ref.shape[0], k_ref.shape[1], k_ref.shape[2]
    w = q_ref.shape[1]
    wk = w // n_k

    def scores(c):
        parts = [_dot(k_ref[ks, c], q_ref[:, ks * wk:(ks + 1) * wk]) for ks in range(n_k)]
        return parts[0] if n_k == 1 else jnp.concatenate(parts, axis=1)

    @pl.when(pl.program_id(2) == 0)
    def _():
        for ks in range(n_k):
            def kbody(c, best):
                kf = k_ref[ks, c].astype(F32)
                return jnp.maximum(best, jnp.max(jnp.sum(kf * kf, axis=-1, keepdims=True), axis=0, keepdims=True))
            k2 = lax.fori_loop(0, n_chunks, kbody, jnp.zeros((1, 1), F32))
            kmax_ref[ks] = jnp.broadcast_to(jnp.sqrt(k2), (1, LANES))

    qf = q_ref[...].astype(F32)
    qn = jnp.sqrt(jnp.sum(qf * qf, axis=0, keepdims=True))
    for ks in range(n_k):
        m_ref[:, ks * wk:(ks + 1) * wk] = qn[:, ks * wk:(ks + 1) * wk] * kmax_ref[ks, :, 0:1]
    acc_ref[...] = jnp.zeros(acc_ref.shape, F32)
    l8_ref[...] = jnp.zeros(l8_ref.shape, F32)

    def fast_body(c, carry):
        p = jnp.exp2(scores(c) - m_ref[...])
        l8_ref[...] += jnp.sum(p.reshape(tk // 8, 8, w), axis=0)
        acc_ref[...] += _dot(v_ref[c], p.astype(BF16))
        return carry

    lax.fori_loop(0, n_chunks, fast_body, 0)
    l_ref[...] = jnp.sum(l8_ref[...], axis=0, keepdims=True)
    underflow = jnp.min(l_ref[...]) < L_MIN_SAFE

    @pl.when(underflow)
    def _():
        acc_ref[...] = jnp.zeros(acc_ref.shape, F32)
        m_ref[...] = jnp.full(m_ref.shape, NEG_BIG, F32)
        l_ref[...] = jnp.zeros(l_ref.shape, F32)

        def body(c, carry):
            s = scores(c)
            m_old = m_ref[...]
            m_new = jnp.maximum(m_old, jnp.max(s, axis=0, keepdims=True))
            alpha = jnp.exp2(m_old - m_new)
            p = jnp.exp2(s - m_new)
            l_ref[...] = alpha * l_ref[...] + jnp.sum(p, axis=0, keepdims=True)
            acc_ref[...] = alpha * acc_ref[...] + _dot(v_ref[c], p.astype(BF16))
            m_ref[...] = m_new
            return carry

        lax.fori_loop(0, n_chunks, body, 0)

    o = acc_ref[...] / l_ref[...]
    if not diff:
        for r in range(w // tq):
            o_ref[:, r * LANES:(r + 1) * LANES] = o[:, r * tq:(r + 1) * tq].T.astype(BF16)
    else:
        lam = (jnp.exp(jnp.sum(lq1_ref[...] * lk1_ref[...], axis=-1, keepdims=True))
               - jnp.exp(jnp.sum(lq2_ref[...] * lk2_ref[...], axis=-1, keepdims=True))
               + lambda_init)
        d = o[:, :tq] - lam * o[:, tq:]
        y = _rms(d.T, subln_ref[...]) * (1.0 - lambda_init)
        o_ref[...] = y.astype(BF16)


def _attn(qt, k, vt, *, tq, diff_args=None, lambda_init=0.0):
    b, g, nqt, hd, w = qt.shape
    n_k, nc, tk = k.shape[2], k.shape[3], k.shape[4]
    dv = vt.shape[3]
    diff = diff_args is not None
    width = dv if diff else (w // tq) * dv
    in_specs = [
        pl.BlockSpec((None, None, None, hd, w), lambda bi, gi, qi: (bi, gi, qi, 0, 0)),
        pl.BlockSpec((None, None, n_k, nc, tk, hd), lambda bi, gi, qi: (bi, gi, 0, 0, 0, 0)),
        pl.BlockSpec((None, None, nc, dv, tk), lambda bi, gi, qi: (bi, gi, 0, 0, 0)),
    ]
    args = [qt, k, vt]
    if diff:
        in_specs += [pl.BlockSpec((1, hd), lambda bi, gi, qi: (0, 0))] * 4
        in_specs += [pl.BlockSpec((1, dv), lambda bi, gi, qi: (0, 0))]
        args += list(diff_args)
    return pl.pallas_call(
        functools.partial(_attn_kernel, tq=tq, diff=diff, lambda_init=lambda_init),
        grid=(b, g, nqt),
        in_specs=in_specs,
        out_specs=pl.BlockSpec((None, tq, width), lambda bi, gi, qi: (bi, qi, gi)),
        out_shape=jax.ShapeDtypeStruct((b, nqt * tq, g * width), BF16),
        scratch_shapes=[
            pltpu.VMEM((dv, w), F32),
            pltpu.VMEM((1, w), F32),
            pltpu.VMEM((1, w), F32),
            pltpu.VMEM((8, w), F32),
            pltpu.VMEM((n_k, 1, LANES), F32),
        ],
        compiler_params=_params("parallel", "parallel", "arbitrary"),
        name="attn_diff" if diff else "attn_gqa",
    )(*args)


def _outproj_kernel(x_ref, a_ref, b_ref, wa_ref, wb_ref, o_ref):
    o_ref[...] = x_ref[...] + _dot(a_ref[...], wa_ref[...]) + _dot(b_ref[...], wb_ref[...])


def _outproj(x, a, b, w_out, *, tm):
    m, d = x.shape
    da, db = a.shape[1], b.shape[1]
    assert da == db and w_out.shape[0] == da + db
    return pl.pallas_call(
        _outproj_kernel,
        grid=(m // tm,),
        in_specs=[
            pl.BlockSpec((tm, d), lambda i: (i, 0)),
            pl.BlockSpec((tm, da), lambda i: (i, 0)),
            pl.BlockSpec((tm, db), lambda i: (i, 0)),
            pl.BlockSpec((da, d), lambda i: (0, 0)),
            pl.BlockSpec((db, d), lambda i: (1, 0)),
        ],
        out_specs=pl.BlockSpec((tm, d), lambda i: (i, 0)),
        out_shape=jax.ShapeDtypeStruct((m, d), F32),
        compiler_params=_params("parallel"),
        name="outproj",
    )(x, a, b, w_out, w_out)


def _rope_cs(pos, dim, theta):
    inv_freq = theta ** (-jnp.arange(0, dim, 2, dtype=F32) / dim)
    ang = pos[:, None] * inv_freq[None, :]
    return jnp.cos(ang), jnp.sin(ang)


ROPE_BLOCK = 16
AXIAL_ORDER = (0, 1, 4, 5, 2, 3, 6, 7)
PARTIAL_ORDER = (0, 2, 3, 4, 1, 5, 6, 7)


def _permute_heads(a, order):
    lead = a.shape[:-1]
    a4 = a.reshape(lead + (a.shape[-1] // HEAD_DIM, HEAD_DIM // ROPE_BLOCK, ROPE_BLOCK))
    return jnp.concatenate([a4[..., i:i + 1, :] for i in order], axis=-2).reshape(a.shape)


def _rope_tables(seq):
    rows = seq // GRID_W
    row_pos = jnp.broadcast_to(jnp.arange(rows, dtype=F32)[:, None], (rows, GRID_W)).reshape(-1)
    col_pos = jnp.broadcast_to(jnp.arange(GRID_W, dtype=F32)[None, :], (rows, GRID_W)).reshape(-1)
    lin_pos = jnp.arange(seq, dtype=F32)
    rc, rs = _rope_cs(row_pos, AXIAL_DIM, A_ROPE_THETA)
    cc, cs = _rope_cs(col_pos, AXIAL_DIM, A_ROPE_THETA)
    pc, ps = _rope_cs(lin_pos, PARTIAL_ROPE_DIM, PARTIAL_ROPE_THETA)
    cos_a = jnp.concatenate([rc, cc, rc, cc], axis=-1)
    sin_a = jnp.concatenate([-rs, -cs, rs, cs], axis=-1)
    rest = HEAD_DIM // 2 - PARTIAL_ROPE_DIM // 2
    one, zero = jnp.ones((seq, rest), F32), jnp.zeros((seq, rest), F32)
    cos_b = jnp.concatenate([pc, one, pc, one], axis=-1)
    sin_b = jnp.concatenate([-ps, zero, ps, zero], axis=-1)
    return (cos_a, sin_a), (cos_b, sin_b)


def _tiles(m, seq, d_ff):
    def fit(n, t):
        t = min(t, n)
        while n % t:
            t //= 2
        return t
    return dict(
        ffn_tm=fit(m, 512), ffn_tf=fit(d_ff, 512),
        proj_tm=fit(seq, 512), out_tm=fit(m, 512),
        a_tq=fit(seq, 256), b_tq=fit(seq, 512), tk=fit(seq, 2048),
    )


def kernel(x, ffn1_norm, ffn1_w_gu, ffn1_w_down, mix_norm, w_in, a_q_norm, a_k_norm, b_q_norm, b_k_norm,
           b_lambda_q1, b_lambda_k1, b_lambda_q2, b_lambda_k2, b_subln, w_out, ffn2_norm, ffn2_w_gu,
           ffn2_w_down, out_norm):
    bsz, seq, d = x.shape
    depth = w_in.shape[0]
    d_ff = ffn1_w_down.shape[1]
    m = bsz * seq
    hd = HEAD_DIM
    a_heads = d // (2 * hd)
    a_kv = a_heads // A_GROUP
    b_vdim = 2 * hd
    b_heads = d // (2 * b_vdim)
    a_q, a_kvw = a_heads * hd, a_kv * hd
    b_qk, b_v = b_heads * 2 * hd, b_heads * b_vdim
    t = _tiles(m, seq, d_ff)
    tk = t["tk"]
    nc = seq // tk
    q_scale = (hd ** -0.5) * LOG2E
    tab_a, tab_b = _rope_tables(seq)

    x = x.reshape(m, d)
    for l in range(depth):
        lambda_init = 0.8 - 0.6 * math.exp(-0.3 * l)
        row = lambda v: v[l].reshape(1, -1).astype(F32)
        w_gu1, w_d1 = ffn1_w_gu[l].astype(BF16), ffn1_w_down[l].astype(BF16)
        w_gu2, w_d2 = ffn2_w_gu[l].astype(BF16), ffn2_w_down[l].astype(BF16)
        w_o = w_out[l].astype(BF16)
        c1 = a_q + a_kvw
        c2 = c1 + a_kvw
        c3 = c2 + 2 * b_qk
        w_i = w_in[l]
        w_qk_a = _permute_heads(w_i[:, :c1], AXIAL_ORDER).astype(BF16)
        w_v_a = w_i[:, c1:c2].astype(BF16)
        w_qk_b = _permute_heads(w_i[:, c2:c3], PARTIAL_ORDER).astype(BF16)
        w_v_b = w_i[:, c3:].astype(BF16)

        x, h = _ffn(x, row(ffn1_norm), w_gu1, w_d1, row(mix_norm),
                    emit_norm=True, final_norm=False, tm=t["ffn_tm"], tf=t["ffn_tf"])

        tqa, tqb, ptm = t["a_tq"], t["b_tq"], t["proj_tm"]
        gain_a = jnp.concatenate([jnp.tile(_permute_heads(row(a_q_norm), AXIAL_ORDER) * q_scale, (1, a_heads)),
                                  jnp.tile(_permute_heads(row(a_k_norm), AXIAL_ORDER), (1, a_kv))], axis=-1)
        qa_t, ka = _proj_qk(h, w_qk_a, gain_a, *tab_a, bsz=bsz, seq=seq, n_qc=a_heads, per_group=A_GROUP,
                            tq=tqa, tm=ptm)
        va_t = _proj_v(h, w_v_a, bsz=bsz, seq=seq, dv=hd, tk=tk, tm=ptm)
        gain_b = jnp.concatenate([jnp.tile(_permute_heads(row(b_q_norm), PARTIAL_ORDER) * q_scale, (1, 2 * b_heads)),
                                  jnp.tile(_permute_heads(row(b_k_norm), PARTIAL_ORDER), (1, 2 * b_heads))], axis=-1)
        qb_t, kb = _proj_qk(h, w_qk_b, gain_b, *tab_b, bsz=bsz, seq=seq, n_qc=2 * b_heads, per_group=2,
                            tq=tqb, tm=ptm)
        vb_t = _proj_v(h, w_v_b, bsz=bsz, seq=seq, dv=b_vdim, tk=tk, tm=ptm)
        ka = ka.reshape(bsz, a_kv, 1, nc, tk, hd)
        kb = kb.reshape(bsz, b_heads, 2, nc, tk, hd)

        out_a = _attn(qa_t, ka, va_t, tq=tqa)
        out_b = _attn(qb_t, kb, vb_t, tq=tqb,
                      diff_args=(row(b_lambda_q1), row(b_lambda_k1), row(b_lambda_q2), row(b_lambda_k2), row(b_subln)),
                      lambda_init=lambda_init)

        x = _outproj(x, out_a.reshape(m, a_q), out_b.reshape(m, b_v), w_o, tm=t["out_tm"])
        x, = _ffn(x, row(ffn2_norm), w_gu2, w_d2, row(out_norm),
                  emit_norm=False, final_norm=True, tm=t["ffn_tm"], tf=t["ffn_tf"])
    return x.reshape(bsz, seq, d)
```

```python
import functools
import math

import jax
import jax.numpy as jnp
from jax import lax
from jax.experimental import pallas as pl
from jax.experimental.pallas import tpu as pltpu

HEAD_DIM = 128
GRID_W = 64
EPS = 1e-6
A_ROPE_THETA = 10000.0
PARTIAL_ROPE_THETA = 500000.0
AXIAL_DIM = HEAD_DIM // 2
PARTIAL_ROPE_DIM = HEAD_DIM // 4
A_GROUP = 4
LANES = 128
LOG2E = math.log2(math.e)
VMEM_LIMIT_BYTES = 56 * 1024 * 1024
NEG_BIG = -1e30
L_MIN_SAFE = 1e-20

BF16 = jnp.bfloat16
F32 = jnp.float32


def _dot(a, b):
    return jnp.dot(a, b, preferred_element_type=F32)


def _rms(x, gain):
    r = lax.rsqrt(jnp.mean(x * x, axis=-1, keepdims=True) + EPS)
    return (x * r) * gain


def _params(*semantics):
    return pltpu.CompilerParams(dimension_semantics=semantics, vmem_limit_bytes=VMEM_LIMIT_BYTES)


def _ffn_kernel(x_ref, gain_ref, wg_ref, wu_ref, wd_ref, ngain_ref, o_ref, *rest, emit_norm, final_norm):
    if emit_norm:
        hn_ref, xn_ref = rest
    else:
        (xn_ref,) = rest
    j = pl.program_id(1)

    @pl.when(j == 0)
    def _():
        x = x_ref[...]
        xn_ref[...] = _rms(x, gain_ref[...]).astype(BF16)
        o_ref[...] = x

    xn = xn_ref[...]
    g = _dot(xn, wg_ref[...])
    u = _dot(xn, wu_ref[...])
    act = (0.5 * g / (1.0 + jnp.exp(-g))) * u
    o_ref[...] += _dot(act.astype(BF16), wd_ref[...])

    @pl.when(j == pl.num_programs(1) - 1)
    def _():
        y = _rms(o_ref[...], ngain_ref[...])
        if emit_norm:
            hn_ref[...] = y.astype(BF16)
        if final_norm:
            o_ref[...] = y


def _ffn(x, gain, w_gu, w_down, next_gain, *, emit_norm, final_norm, tm, tf):
    m, d = x.shape
    d_ff = w_down.shape[0]
    nf = d_ff // tf
    assert m % tm == 0 and d_ff % tf == 0
    out_shape = [jax.ShapeDtypeStruct((m, d), F32)]
    out_specs = [pl.BlockSpec((tm, d), lambda i, j: (i, 0))]
    if emit_norm:
        out_shape.append(jax.ShapeDtypeStruct((m, d), BF16))
        out_specs.append(pl.BlockSpec((tm, d), lambda i, j: (i, 0)))
    res = pl.pallas_call(
        functools.partial(_ffn_kernel, emit_norm=emit_norm, final_norm=final_norm),
        grid=(m // tm, nf),
        in_specs=[
            pl.BlockSpec((tm, d), lambda i, j: (i, 0)),
            pl.BlockSpec((1, d), lambda i, j: (0, 0)),
            pl.BlockSpec((d, tf), lambda i, j: (0, j)),
            pl.BlockSpec((d, tf), lambda i, j: (0, j + nf)),
            pl.BlockSpec((tf, d), lambda i, j: (j, 0)),
            pl.BlockSpec((1, d), lambda i, j: (0, 0)),
        ],
        out_specs=out_specs,
        out_shape=out_shape,
        scratch_shapes=[pltpu.VMEM((tm, d), BF16)],
        compiler_params=_params("parallel", "arbitrary"),
        name="ffn",
    )(x, gain, w_gu, w_gu, w_down, next_gain)
    return res


PROJ_COLS = 2 * LANES


def _proj_kernel(h_ref, w_ref, gain_ref, cos_ref, sin_ref, q_ref, k_ref, v_ref, acc_a, acc_b, *,
                 n_qc, n_kc, per_group, tq):
    i = pl.program_id(0)
    tm = h_ref.shape[0]
    width = w_ref.shape[1]
    dv = v_ref.shape[1]
    v0 = (n_qc + n_kc) * LANES

    @pl.when(i == 0)
    def _():
        acc_b[...] = jnp.zeros(acc_b.shape, F32)

    def step(mm_ref, ep_ref):
        for c0 in range(0, width, PROJ_COLS):
            cols = min(PROJ_COLS, width - c0)
            mm_ref[:, c0:c0 + cols] = _dot(h_ref[...], w_ref[:, c0:c0 + cols])
        cos, sin = cos_ref[...], sin_ref[...]
        for c in range(n_qc + n_kc):
            sl = slice(c * LANES, (c + 1) * LANES)
            y = _rms(ep_ref[:, sl], gain_ref[:, sl])
            y = y * cos + pltpu.roll(y, LANES // 2, 1) * sin
            if c < n_qc:
                g, r = divmod(c, per_group)
                yt = y.T.astype(BF16)
                for j in range(tm // tq):
                    q_ref[g, j, :, r * tq:(r + 1) * tq] = yt[:, j * tq:(j + 1) * tq]
            else:
                k_ref[c - n_qc] = y.astype(BF16)
        for g in range((width - v0) // dv):
            v_ref[g] = ep_ref[:, v0 + g * dv:v0 + (g + 1) * dv].T.astype(BF16)

    @pl.when(i % 2 == 0)
    def _():
        step(acc_a, acc_b)

    @pl.when(i % 2 == 1)
    def _():
        step(acc_b, acc_a)


def _proj(h, w, gain, cos, sin, *, bsz, seq, n_qc, n_kc, per_group, dv, tq, tk, tm):
    m, d = h.shape
    width = w.shape[1]
    qk_w = (n_qc + n_kc) * LANES
    groups, v_groups = n_qc // per_group, (width - qk_w) // dv
    nt, per_chunk, n = seq // tm, tk // tm, m // tm
    assert tm % tq == 0 and tk % tm == 0 and seq % tk == 0 and (width - qk_w) % dv == 0

    def cur(i):
        return jnp.minimum(i, n - 1)

    def prev(i):
        return jnp.maximum(i - 1, 0)

    return pl.pallas_call(
        functools.partial(_proj_kernel, n_qc=n_qc, n_kc=n_kc, per_group=per_group, tq=tq),
        grid=(n + 1,),
        in_specs=[
            pl.BlockSpec((tm, d), lambda i: (cur(i), 0)),
            pl.BlockSpec((d, width), lambda i: (0, 0)),
            pl.BlockSpec((1, qk_w), lambda i: (0, 0)),
            pl.BlockSpec((tm, LANES), lambda i: (prev(i) % nt, 0)),
            pl.BlockSpec((tm, LANES), lambda i: (prev(i) % nt, 0)),
        ],
        out_specs=[
            pl.BlockSpec((None, groups, tm // tq, LANES, per_group * tq),
                         lambda i: (prev(i) // nt, 0, prev(i) % nt, 0, 0)),
            pl.BlockSpec((None, n_kc, tm, LANES), lambda i: (prev(i) // nt, 0, prev(i) % nt, 0)),
            pl.BlockSpec((None, v_groups, None, dv, tm),
                         lambda i: (prev(i) // nt, 0, (prev(i) % nt) // per_chunk, 0, (prev(i) % nt) % per_chunk)),
        ],
        out_shape=[
            jax.ShapeDtypeStruct((bsz, groups, seq // tq, LANES, per_group * tq), BF16),
            jax.ShapeDtypeStruct((bsz, n_kc, seq, LANES), BF16),
            jax.ShapeDtypeStruct((bsz, v_groups, seq // tk, dv, tk), BF16),
        ],
        scratch_shapes=[pltpu.VMEM((tm, width), F32), pltpu.VMEM((tm, width), F32)],
        compiler_params=_params("arbitrary"),
        name="proj",
    )(h, w, gain, cos, sin)


def _attn_kernel(*refs, tq, diff, lambda_init):
    if diff:
        q_ref, k_ref, v_ref, lq1_ref, lk1_ref, lq2_ref, lk2_ref, subln_ref, o_ref, acc_ref, m_ref, l_ref, l8_ref, kmax_ref = refs
    else:
        q_ref, k_ref, v_ref, o_ref, acc_ref, m_ref, l_ref, l8_ref, kmax_ref = refs
    n_k, n_chunks, tk = k_ref.shape[0], k_ref.shape[1], k_ref.shape[2]
    w = q_ref.shape[1]
    wk = w // n_k

    def scores(c):
        parts = [_dot(k_ref[ks, c], q_ref[:, ks * wk:(ks + 1) * wk]) for ks in range(n_k)]
        return parts[0] if n_k == 1 else jnp.concatenate(parts, axis=1)

    @pl.when(pl.program_id(2) == 0)
    def _():
        for ks in range(n_k):
            def kbody(c, best):
                kf = k_ref[ks, c].astype(F32)
                return jnp.maximum(best, jnp.max(jnp.sum(kf * kf, axis=-1, keepdims=True), axis=0, keepdims=True))
            k2 = lax.fori_loop(0, n_chunks, kbody, jnp.zeros((1, 1), F32))
            kmax_ref[ks] = jnp.broadcast_to(jnp.sqrt(k2), (1, LANES))

    qf = q_ref[...].astype(F32)
    qn = jnp.sqrt(jnp.sum(qf * qf, axis=0, keepdims=True))
    for ks in range(n_k):
        m_ref[:, ks * wk:(ks + 1) * wk] = qn[:, ks * wk:(ks + 1) * wk] * kmax_ref[ks, :, 0:1]
    acc_ref[...] = jnp.zeros(acc_ref.shape, F32)
    l8_ref[...] = jnp.zeros(l8_ref.shape, F32)

    def fast_body(c, carry):
        p = jnp.exp2(scores(c) - m_ref[...])
        l8_ref[...] += jnp.sum(p.reshape(tk // 8, 8, w), axis=0)
        acc_ref[...] += _dot(v_ref[c], p.astype(BF16))
        return carry

    lax.fori_loop(0, n_chunks, fast_body, 0)
    l_ref[...] = jnp.sum(l8_ref[...], axis=0, keepdims=True)
    underflow = jnp.min(l_ref[...]) < L_MIN_SAFE

    @pl.when(underflow)
    def _():
        acc_ref[...] = jnp.zeros(acc_ref.shape, F32)
        m_ref[...] = jnp.full(m_ref.shape, NEG_BIG, F32)
        l_ref[...] = jnp.zeros(l_ref.shape, F32)

        def body(c, carry):
            s = scores(c)
            m_old = m_ref[...]
            m_new = jnp.maximum(m_old, jnp.max(s, axis=0, keepdims=True))
            alpha = jnp.exp2(m_old - m_new)
            p = jnp.exp2(s - m_new)
            l_ref[...] = alpha * l_ref[...] + jnp.sum(p, axis=0, keepdims=True)
            acc_ref[...] = alpha * acc_ref[...] + _dot(v_ref[c], p.astype(BF16))
            m_ref[...] = m_new
            return carry

        lax.fori_loop(0, n_chunks, body, 0)

    o = acc_ref[...] / l_ref[...]
    if not diff:
        for r in range(w // tq):
            o_ref[:, r * LANES:(r + 1) * LANES] = o[:, r * tq:(r + 1) * tq].T.astype(BF16)
    else:
        lam = (jnp.exp(jnp.sum(lq1_ref[...] * lk1_ref[...], axis=-1, keepdims=True))
               - jnp.exp(jnp.sum(lq2_ref[...] * lk2_ref[...], axis=-1, keepdims=True))
               + lambda_init)
        d = o[:, :tq] - lam * o[:, tq:]
        y = _rms(d.T, subln_ref[...]) * (1.0 - lambda_init)
        o_ref[...] = y.astype(BF16)


def _attn(qt, k, vt, *, tq, diff_args=None, lambda_init=0.0):
    b, g, nqt, hd, w = qt.shape
    n_k, nc, tk = k.shape[2], k.shape[3], k.shape[4]
    dv = vt.shape[3]
    diff = diff_args is not None
    width = dv if diff else (w // tq) * dv
    in_specs = [
        pl.BlockSpec((None, None, None, hd, w), lambda bi, gi, qi: (bi, gi, qi, 0, 0)),
        pl.BlockSpec((None, None, n_k, nc, tk, hd), lambda bi, gi, qi: (bi, gi, 0, 0, 0, 0)),
        pl.BlockSpec((None, None, nc, dv, tk), lambda bi, gi, qi: (bi, gi, 0, 0, 0)),
    ]
    args = [qt, k, vt]
    if diff:
        in_specs += [pl.BlockSpec((1, hd), lambda bi, gi, qi: (0, 0))] * 4
        in_specs += [pl.BlockSpec((1, dv), lambda bi, gi, qi: (0, 0))]
        args += list(diff_args)
    return pl.pallas_call(
        functools.partial(_attn_kernel, tq=tq, diff=diff, lambda_init=lambda_init),
        grid=(b, g, nqt),
        in_specs=in_specs,
        out_specs=pl.BlockSpec((None, tq, width), lambda bi, gi, qi: (bi, qi, gi)),
        out_shape=jax.ShapeDtypeStruct((b, nqt * tq, g * width), BF16),
        scratch_shapes=[
            pltpu.VMEM((dv, w), F32),
            pltpu.VMEM((1, w), F32),
            pltpu.VMEM((1, w), F32),
            pltpu.VMEM((8, w), F32),
            pltpu.VMEM((n_k, 1, LANES), F32),
        ],
        compiler_params=_params("parallel", "parallel", "arbitrary"),
        name="attn_diff" if diff else "attn_gqa",
    )(*args)


def _outproj_kernel(x_ref, a_ref, b_ref, wa_ref, wb_ref, o_ref):
    o_ref[...] = x_ref[...] + _dot(a_ref[...], wa_ref[...]) + _dot(b_ref[...], wb_ref[...])


def _outproj(x, a, b, w_out, *, tm):
    m, d = x.shape
    da, db = a.shape[1], b.shape[1]
    assert da == db and w_out.shape[0] == da + db
    return pl.pallas_call(
        _outproj_kernel,
        grid=(m // tm,),
        in_specs=[
            pl.BlockSpec((tm, d), lambda i: (i, 0)),
            pl.BlockSpec((tm, da), lambda i: (i, 0)),
            pl.BlockSpec((tm, db), lambda i: (i, 0)),
            pl.BlockSpec((da, d), lambda i: (0, 0)),
            pl.BlockSpec((db, d), lambda i: (1, 0)),
        ],
        out_specs=pl.BlockSpec((tm, d), lambda i: (i, 0)),
        out_shape=jax.ShapeDtypeStruct((m, d), F32),
        compiler_params=_params("parallel"),
        name="outproj",
    )(x, a, b, w_out, w_out)


def _rope_cs(pos, dim, theta):
    inv_freq = theta ** (-jnp.arange(0, dim, 2, dtype=F32) / dim)
    ang = pos[:, None] * inv_freq[None, :]
    return jnp.cos(ang), jnp.sin(ang)


ROPE_BLOCK = 16
AXIAL_ORDER = (0, 1, 4, 5, 2, 3, 6, 7)
PARTIAL_ORDER = (0, 2, 3, 4, 1, 5, 6, 7)


def _permute_heads(a, order):
    lead = a.shape[:-1]
    a4 = a.reshape(lead + (a.shape[-1] // HEAD_DIM, HEAD_DIM // ROPE_BLOCK, ROPE_BLOCK))
    return jnp.concatenate([a4[..., i:i + 1, :] for i in order], axis=-2).reshape(a.shape)


def _rope_tables(seq):
    rows = seq // GRID_W
    row_pos = jnp.broadcast_to(jnp.arange(rows, dtype=F32)[:, None], (rows, GRID_W)).reshape(-1)
    col_pos = jnp.broadcast_to(jnp.arange(GRID_W, dtype=F32)[None, :], (rows, GRID_W)).reshape(-1)
    lin_pos = jnp.arange(seq, dtype=F32)
    rc, rs = _rope_cs(row_pos, AXIAL_DIM, A_ROPE_THETA)
    cc, cs = _rope_cs(col_pos, AXIAL_DIM, A_ROPE_THETA)
    pc, ps = _rope_cs(lin_pos, PARTIAL_ROPE_DIM, PARTIAL_ROPE_THETA)
    cos_a = jnp.concatenate([rc, cc, rc, cc], axis=-1)
    sin_a = jnp.concatenate([-rs, -cs, rs, cs], axis=-1)
    rest = HEAD_DIM // 2 - PARTIAL_ROPE_DIM // 2
    one, zero = jnp.ones((seq, rest), F32), jnp.zeros((seq, rest), F32)
    cos_b = jnp.concatenate([pc, one, pc, one], axis=-1)
    sin_b = jnp.concatenate([-ps, zero, ps, zero], axis=-1)
    return (cos_a, sin_a), (cos_b, sin_b)


def _tiles(m, seq, d_ff):
    def fit(n, t):
        t = min(t, n)
        while n % t:
            t //= 2
        return t
    return dict(
        ffn_tm=fit(m, 512), ffn_tf=fit(d_ff, 512),
        proj_tm=fit(seq, 512), out_tm=fit(m, 512),
        a_tq=fit(seq, 256), b_tq=fit(seq, 512), tk=fit(seq, 4096),
    )


def kernel(x, ffn1_norm, ffn1_w_gu, ffn1_w_down, mix_norm, w_in, a_q_norm, a_k_norm, b_q_norm, b_k_norm,
           b_lambda_q1, b_lambda_k1, b_lambda_q2, b_lambda_k2, b_subln, w_out, ffn2_norm, ffn2_w_gu,
           ffn2_w_down, out_norm):
    bsz, seq, d = x.shape
    depth = w_in.shape[0]
    d_ff = ffn1_w_down.shape[1]
    m = bsz * seq
    hd = HEAD_DIM
    a_heads = d // (2 * hd)
    a_kv = a_heads // A_GROUP
    b_vdim = 2 * hd
    b_heads = d // (2 * b_vdim)
    a_q, a_kvw = a_heads * hd, a_kv * hd
    b_qk, b_v = b_heads * 2 * hd, b_heads * b_vdim
    t = _tiles(m, seq, d_ff)
    tk = t["tk"]
    nc = seq // tk
    q_scale = (hd ** -0.5) * LOG2E
    tab_a, tab_b = _rope_tables(seq)

    x = x.reshape(m, d)
    for l in range(depth):
        lambda_init = 0.8 - 0.6 * math.exp(-0.3 * l)
        row = lambda v: v[l].reshape(1, -1).astype(F32)
        w_gu1, w_d1 = ffn1_w_gu[l].astype(BF16), ffn1_w_down[l].astype(BF16)
        w_gu2, w_d2 = ffn2_w_gu[l].astype(BF16), ffn2_w_down[l].astype(BF16)
        w_o = w_out[l].astype(BF16)
        c1 = a_q + a_kvw
        c2 = c1 + a_kvw
        c3 = c2 + 2 * b_qk
        w_i = w_in[l]
        w_a = jnp.concatenate([_permute_heads(w_i[:, :c1], AXIAL_ORDER), w_i[:, c1:c2]], axis=1).astype(BF16)
        w_b = jnp.concatenate([_permute_heads(w_i[:, c2:c3], PARTIAL_ORDER), w_i[:, c3:]], axis=1).astype(BF16)

        x, h = _ffn(x, row(ffn1_norm), w_gu1, w_d1, row(mix_norm),
                    emit_norm=True, final_norm=False, tm=t["ffn_tm"], tf=t["ffn_tf"])

        tqa, tqb, ptm = t["a_tq"], t["b_tq"], t["proj_tm"]
        gain_a = jnp.concatenate([jnp.tile(_permute_heads(row(a_q_norm), AXIAL_ORDER) * q_scale, (1, a_heads)),
                                  jnp.tile(_permute_heads(row(a_k_norm), AXIAL_ORDER), (1, a_kv))], axis=-1)
        qa_t, ka, va_t = _proj(h, w_a, gain_a, *tab_a, bsz=bsz, seq=seq, n_qc=a_heads, n_kc=a_kv,
                               per_group=A_GROUP, dv=hd, tq=tqa, tk=tk, tm=ptm)
        gain_b = jnp.concatenate([jnp.tile(_permute_heads(row(b_q_norm), PARTIAL_ORDER) * q_scale, (1, 2 * b_heads)),
                                  jnp.tile(_permute_heads(row(b_k_norm), PARTIAL_ORDER), (1, 2 * b_heads))], axis=-1)
        qb_t, kb, vb_t = _proj(h, w_b, gain_b, *tab_b, bsz=bsz, seq=seq, n_qc=2 * b_heads, n_kc=2 * b_heads,
                               per_group=2, dv=b_vdim, tq=tqb, tk=tk, tm=ptm)
        ka = ka.reshape(bsz, a_kv, 1, nc, tk, hd)
        kb = kb.reshape(bsz, b_heads, 2, nc, tk, hd)

        out_a = _attn(qa_t, ka, va_t, tq=tqa)
        out_b = _attn(qb_t, kb, vb_t, tq=tqb,
                      diff_args=(row(b_lambda_q1), row(b_lambda_k1), row(b_lambda_q2), row(b_lambda_k2), row(b_subln)),
                      lambda_init=lambda_init)

        x = _outproj(x, out_a.reshape(m, a_q), out_b.reshape(m, b_v), w_o, tm=t["out_tm"])
        x, = _ffn(x, row(ffn2_norm), w_gu2, w_d2, row(out_norm),
                  emit_norm=False, final_norm=True, tm=t["ffn_tm"], tf=t["ffn_tf"])
    return x.reshape(bsz, seq, d)
```

```python
import functools
import math

import jax
import jax.numpy as jnp
from jax import lax
from jax.experimental import pallas as pl
from jax.experimental.pallas import tpu as pltpu

HEAD_DIM = 128
GRID_W = 64
EPS = 1e-6
A_ROPE_THETA = 10000.0
PARTIAL_ROPE_THETA = 500000.0
AXIAL_DIM = HEAD_DIM // 2
PARTIAL_ROPE_DIM = HEAD_DIM // 4
A_GROUP = 4
LANES = 128
LOG2E = math.log2(math.e)
VMEM_LIMIT_BYTES = 60 * 1024 * 1024
NEG_BIG = -1e30
L_MIN_SAFE = 1e-20

BF16 = jnp.bfloat16
F32 = jnp.float32


def _dot(a, b):
    return jnp.dot(a, b, preferred_element_type=F32)


def _rms(x, gain):
    r = lax.rsqrt(jnp.mean(x * x, axis=-1, keepdims=True) + EPS)
    return (x * r) * gain


def _params(*semantics):
    return pltpu.CompilerParams(dimension_semantics=semantics, vmem_limit_bytes=VMEM_LIMIT_BYTES)


def _ffn_kernel(x_ref, gain_ref, wg_ref, wu_ref, wd_ref, ngain_ref, o_ref, *rest, emit_norm, final_norm):
    if emit_norm:
        hn_ref, xn_ref = rest
    else:
        (xn_ref,) = rest
    j = pl.program_id(1)

    @pl.when(j == 0)
    def _():
        x = x_ref[...]
        xn_ref[...] = _rms(x, gain_ref[...]).astype(BF16)
        o_ref[...] = x

    xn = xn_ref[...]
    g = _dot(xn, wg_ref[...])
    u = _dot(xn, wu_ref[...])
    act = (0.5 * g / (1.0 + jnp.exp(-g))) * u
    o_ref[...] += _dot(act.astype(BF16), wd_ref[...])

    @pl.when(j == pl.num_programs(1) - 1)
    def _():
        y = _rms(o_ref[...], ngain_ref[...])
        if emit_norm:
            hn_ref[...] = y.astype(BF16)
        if final_norm:
            o_ref[...] = y


def _ffn(x, gain, w_gu, w_down, next_gain, *, emit_norm, final_norm, tm, tf):
    m, d = x.shape
    d_ff = w_down.shape[0]
    nf = d_ff // tf
    assert m % tm == 0 and d_ff % tf == 0
    out_shape = [jax.ShapeDtypeStruct((m, d), F32)]
    out_specs = [pl.BlockSpec((tm, d), lambda i, j: (i, 0))]
    if emit_norm:
        out_shape.append(jax.ShapeDtypeStruct((m, d), BF16))
        out_specs.append(pl.BlockSpec((tm, d), lambda i, j: (i, 0)))
    res = pl.pallas_call(
        functools.partial(_ffn_kernel, emit_norm=emit_norm, final_norm=final_norm),
        grid=(m // tm, nf),
        in_specs=[
            pl.BlockSpec((tm, d), lambda i, j: (i, 0)),
            pl.BlockSpec((1, d), lambda i, j: (0, 0)),
            pl.BlockSpec((d, tf), lambda i, j: (0, j)),
            pl.BlockSpec((d, tf), lambda i, j: (0, j + nf)),
            pl.BlockSpec((tf, d), lambda i, j: (j, 0)),
            pl.BlockSpec((1, d), lambda i, j: (0, 0)),
        ],
        out_specs=out_specs,
        out_shape=out_shape,
        scratch_shapes=[pltpu.VMEM((tm, d), BF16)],
        compiler_params=_params("parallel", "arbitrary"),
        name="ffn",
    )(x, gain, w_gu, w_gu, w_down, next_gain)
    return res


PROJ_COLS = 2 * LANES


def _proj_kernel(h_ref, w_ref, gain_ref, cos_ref, sin_ref, q_ref, k_ref, v_ref, acc_a, acc_b, *,
                 n_qc, n_kc, per_group, tq):
    i = pl.program_id(0)
    tm = h_ref.shape[0]
    width = w_ref.shape[1]
    dv = v_ref.shape[1]
    v0 = (n_qc + n_kc) * LANES

    @pl.when(i == 0)
    def _():
        acc_b[...] = jnp.zeros(acc_b.shape, F32)

    def step(mm_ref, ep_ref):
        for c0 in range(0, width, PROJ_COLS):
            cols = min(PROJ_COLS, width - c0)
            mm_ref[:, c0:c0 + cols] = _dot(h_ref[...], w_ref[:, c0:c0 + cols])
        cos, sin = cos_ref[...], sin_ref[...]
        for c in range(n_qc + n_kc):
            sl = slice(c * LANES, (c + 1) * LANES)
            y = _rms(ep_ref[:, sl], gain_ref[:, sl])
            y = y * cos + pltpu.roll(y, LANES // 2, 1) * sin
            if c < n_qc:
                g, r = divmod(c, per_group)
                yt = y.T.astype(BF16)
                for j in range(tm // tq):
                    q_ref[g, j, :, r * tq:(r + 1) * tq] = yt[:, j * tq:(j + 1) * tq]
            else:
                k_ref[c - n_qc] = y.astype(BF16)
        for g in range((width - v0) // dv):
            v_ref[g] = ep_ref[:, v0 + g * dv:v0 + (g + 1) * dv].T.astype(BF16)

    @pl.when(i % 2 == 0)
    def _():
        step(acc_a, acc_b)

    @pl.when(i % 2 == 1)
    def _():
        step(acc_b, acc_a)


def _proj(h, w, gain, cos, sin, *, bsz, seq, n_qc, n_kc, per_group, dv, tq, tk, tm):
    m, d = h.shape
    width = w.shape[1]
    qk_w = (n_qc + n_kc) * LANES
    groups, v_groups = n_qc // per_group, (width - qk_w) // dv
    nt, per_chunk, n = seq // tm, tk // tm, m // tm
    assert tm % tq == 0 and tk % tm == 0 and seq % tk == 0 and (width - qk_w) % dv == 0

    def cur(i):
        return jnp.minimum(i, n - 1)

    def prev(i):
        return jnp.maximum(i - 1, 0)

    return pl.pallas_call(
        functools.partial(_proj_kernel, n_qc=n_qc, n_kc=n_kc, per_group=per_group, tq=tq),
        grid=(n + 1,),
        in_specs=[
            pl.BlockSpec((tm, d), lambda i: (cur(i), 0)),
            pl.BlockSpec((d, width), lambda i: (0, 0)),
            pl.BlockSpec((1, qk_w), lambda i: (0, 0)),
            pl.BlockSpec((tm, LANES), lambda i: (prev(i) % nt, 0)),
            pl.BlockSpec((tm, LANES), lambda i: (prev(i) % nt, 0)),
        ],
        out_specs=[
            pl.BlockSpec((None, groups, tm // tq, LANES, per_group * tq),
                         lambda i: (prev(i) // nt, 0, prev(i) % nt, 0, 0)),
            pl.BlockSpec((None, n_kc, tm, LANES), lambda i: (prev(i) // nt, 0, prev(i) % nt, 0)),
            pl.BlockSpec((None, v_groups, None, dv, tm),
                         lambda i: (prev(i) // nt, 0, (prev(i) % nt) // per_chunk, 0, (prev(i) % nt) % per_chunk)),
        ],
        out_shape=[
            jax.ShapeDtypeStruct((bsz, groups, seq // tq, LANES, per_group * tq), BF16),
            jax.ShapeDtypeStruct((bsz, n_kc, seq, LANES), BF16),
            jax.ShapeDtypeStruct((bsz, v_groups, seq // tk, dv, tk), BF16),
        ],
        scratch_shapes=[pltpu.VMEM((tm, width), F32), pltpu.VMEM((tm, width), F32)],
        compiler_params=_params("arbitrary"),
        name="proj",
    )(h, w, gain, cos, sin)


def _attn_kernel(*refs, tq, diff, lambda_init):
    if diff:
        q_ref, k_ref, v_ref, lq1_ref, lk1_ref, lq2_ref, lk2_ref, subln_ref, o_ref, acc_ref, m_ref, l_ref, l8_ref, kmax_ref = refs
    else:
        q_ref, k_ref, v_ref, o_ref, acc_ref, m_ref, l_ref, l8_ref, kmax_ref = refs
    n_k, n_chunks, tk = k_ref.shape[0], k_ref.shape[1], k_ref.shape[2]
    w = q_ref.shape[1]
    wk = w // n_k

    def scores(c):
        parts = [_dot(k_ref[ks, c], q_ref[:, ks * wk:(ks + 1) * wk]) for ks in range(n_k)]
        return parts[0] if n_k == 1 else jnp.concatenate(parts, axis=1)

    @pl.when(pl.program_id(2) == 0)
    def _():
        for ks in range(n_k):
            def kbody(c, best):
                kf = k_ref[ks, c].astype(F32)
                return jnp.maximum(best, jnp.max(jnp.sum(kf * kf, axis=-1, keepdims=True), axis=0, keepdims=True))
            k2 = lax.fori_loop(0, n_chunks, kbody, jnp.zeros((1, 1), F32))
            kmax_ref[ks] = jnp.broadcast_to(jnp.sqrt(k2), (1, LANES))

    qf = q_ref[...].astype(F32)
    qn = jnp.sqrt(jnp.sum(qf * qf, axis=0, keepdims=True))
    for ks in range(n_k):
        m_ref[:, ks * wk:(ks + 1) * wk] = qn[:, ks * wk:(ks + 1) * wk] * kmax_ref[ks, :, 0:1]
    acc_ref[...] = jnp.zeros(acc_ref.shape, F32)
    l8_ref[...] = jnp.zeros(l8_ref.shape, F32)

    def fast_body(c, carry):
        p = jnp.exp2(scores(c) - m_ref[...])
        l8_ref[...] += jnp.sum(p.reshape(tk // 8, 8, w), axis=0)
        acc_ref[...] += _dot(v_ref[c], p.astype(BF16))
        return carry

    lax.fori_loop(0, n_chunks, fast_body, 0)
    l_ref[...] = jnp.sum(l8_ref[...], axis=0, keepdims=True)
    underflow = jnp.min(l_ref[...]) < L_MIN_SAFE

    @pl.when(underflow)
    def _():
        acc_ref[...] = jnp.zeros(acc_ref.shape, F32)
        m_ref[...] = jnp.full(m_ref.shape, NEG_BIG, F32)
        l_ref[...] = jnp.zeros(l_ref.shape, F32)

        def body(c, carry):
            s = scores(c)
            m_old = m_ref[...]
            m_new = jnp.maximum(m_old, jnp.max(s, axis=0, keepdims=True))
            alpha = jnp.exp2(m_old - m_new)
            p = jnp.exp2(s - m_new)
            l_ref[...] = alpha * l_ref[...] + jnp.sum(p, axis=0, keepdims=True)
            acc_ref[...] = alpha * acc_ref[...] + _dot(v_ref[c], p.astype(BF16))
            m_ref[...] = m_new
            return carry

        lax.fori_loop(0, n_chunks, body, 0)

    o = acc_ref[...] / l_ref[...]
    if not diff:
        for r in range(w // tq):
            o_ref[:, r * LANES:(r + 1) * LANES] = o[:, r * tq:(r + 1) * tq].T.astype(BF16)
    else:
        lam = (jnp.exp(jnp.sum(lq1_ref[...] * lk1_ref[...], axis=-1, keepdims=True))
               - jnp.exp(jnp.sum(lq2_ref[...] * lk2_ref[...], axis=-1, keepdims=True))
               + lambda_init)
        d = o[:, :tq] - lam * o[:, tq:]
        y = _rms(d.T, subln_ref[...]) * (1.0 - lambda_init)
        o_ref[...] = y.astype(BF16)


def _attn(qt, k, vt, *, tq, diff_args=None, lambda_init=0.0):
    b, g, nqt, hd, w = qt.shape
    n_k, nc, tk = k.shape[2], k.shape[3], k.shape[4]
    dv = vt.shape[3]
    diff = diff_args is not None
    width = dv if diff else (w // tq) * dv
    in_specs = [
        pl.BlockSpec((None, None, None, hd, w), lambda bi, gi, qi: (bi, gi, qi, 0, 0)),
        pl.BlockSpec((None, None, n_k, nc, tk, hd), lambda bi, gi, qi: (bi, gi, 0, 0, 0, 0)),
        pl.BlockSpec((None, None, nc, dv, tk), lambda bi, gi, qi: (bi, gi, 0, 0, 0)),
    ]
    args = [qt, k, vt]
    if diff:
        in_specs += [pl.BlockSpec((1, hd), lambda bi, gi, qi: (0, 0))] * 4
        in_specs += [pl.BlockSpec((1, dv), lambda bi, gi, qi: (0, 0))]
        args += list(diff_args)
    return pl.pallas_call(
        functools.partial(_attn_kernel, tq=tq, diff=diff, lambda_init=lambda_init),
        grid=(b, g, nqt),
        in_specs=in_specs,
        out_specs=pl.BlockSpec((None, tq, width), lambda bi, gi, qi: (bi, qi, gi)),
        out_shape=jax.ShapeDtypeStruct((b, nqt * tq, g * width), BF16),
        scratch_shapes=[
            pltpu.VMEM((dv, w), F32),
            pltpu.VMEM((1, w), F32),
            pltpu.VMEM((1, w), F32),
            pltpu.VMEM((8, w), F32),
            pltpu.VMEM((n_k, 1, LANES), F32),
        ],
        compiler_params=_params("parallel", "parallel", "arbitrary"),
        name="attn_diff" if diff else "attn_gqa",
    )(*args)


def _outproj_kernel(x_ref, a_ref, b_ref, wa_ref, wb_ref, o_ref):
    o_ref[...] = x_ref[...] + _dot(a_ref[...], wa_ref[...]) + _dot(b_ref[...], wb_ref[...])


def _outproj(x, a, b, w_out, *, tm):
    m, d = x.shape
    da, db = a.shape[1], b.shape[1]
    assert da == db and w_out.shape[0] == da + db
    return pl.pallas_call(
        _outproj_kernel,
        grid=(m // tm,),
        in_specs=[
            pl.BlockSpec((tm, d), lambda i: (i, 0)),
            pl.BlockSpec((tm, da), lambda i: (i, 0)),
            pl.BlockSpec((tm, db), lambda i: (i, 0)),
            pl.BlockSpec((da, d), lambda i: (0, 0)),
            pl.BlockSpec((db, d), lambda i: (1, 0)),
        ],
        out_specs=pl.BlockSpec((tm, d), lambda i: (i, 0)),
        out_shape=jax.ShapeDtypeStruct((m, d), F32),
        compiler_params=_params("parallel"),
        name="outproj",
    )(x, a, b, w_out, w_out)


def _rope_cs(pos, dim, theta):
    inv_freq = theta ** (-jnp.arange(0, dim, 2, dtype=F32) / dim)
    ang = pos[:, None] * inv_freq[None, :]
    return jnp.cos(ang), jnp.sin(ang)


ROPE_BLOCK = 16
AXIAL_ORDER = (0, 1, 4, 5, 2, 3, 6, 7)
PARTIAL_ORDER = (0, 2, 3, 4, 1, 5, 6, 7)


def _permute_heads(a, order):
    lead = a.shape[:-1]
    a4 = a.reshape(lead + (a.shape[-1] // HEAD_DIM, HEAD_DIM // ROPE_BLOCK, ROPE_BLOCK))
    return jnp.concatenate([a4[..., i:i + 1, :] for i in order], axis=-2).reshape(a.shape)


def _rope_tables(seq):
    rows = seq // GRID_W
    rc, rs = (jnp.repeat(a, GRID_W, axis=0) for a in _rope_cs(jnp.arange(rows, dtype=F32), AXIAL_DIM, A_ROPE_THETA))
    cc, cs = (jnp.tile(a, (rows, 1)) for a in _rope_cs(jnp.arange(GRID_W, dtype=F32), AXIAL_DIM, A_ROPE_THETA))
    pc, ps = _rope_cs(jnp.arange(seq, dtype=F32), PARTIAL_ROPE_DIM, PARTIAL_ROPE_THETA)
    cos_a = jnp.concatenate([rc, cc, rc, cc], axis=-1)
    sin_a = jnp.concatenate([-rs, -cs, rs, cs], axis=-1)
    rest = HEAD_DIM // 2 - PARTIAL_ROPE_DIM // 2
    one, zero = jnp.ones((seq, rest), F32), jnp.zeros((seq, rest), F32)
    cos_b = jnp.concatenate([pc, one, pc, one], axis=-1)
    sin_b = jnp.concatenate([-ps, zero, ps, zero], axis=-1)
    return (cos_a, sin_a), (cos_b, sin_b)


def _tiles(m, seq, d_ff):
    def fit(n, t):
        t = min(t, n)
        while n % t:
            t //= 2
        return t
    return dict(
        ffn1_tm=fit(m, 512), ffn2_tm=fit(m, 1024), ffn_tf=fit(d_ff, 512),
        proj_tm=fit(seq, 512), out_tm=fit(m, 512),
        a_tq=fit(seq, 256), b_tq=fit(seq, 512), tk=fit(seq, 4096),
    )


def kernel(x, ffn1_norm, ffn1_w_gu, ffn1_w_down, mix_norm, w_in, a_q_norm, a_k_norm, b_q_norm, b_k_norm,
           b_lambda_q1, b_lambda_k1, b_lambda_q2, b_lambda_k2, b_subln, w_out, ffn2_norm, ffn2_w_gu,
           ffn2_w_down, out_norm):
    bsz, seq, d = x.shape
    depth = w_in.shape[0]
    d_ff = ffn1_w_down.shape[1]
    m = bsz * seq
    hd = HEAD_DIM
    a_heads = d // (2 * hd)
    a_kv = a_heads // A_GROUP
    b_vdim = 2 * hd
    b_heads = d // (2 * b_vdim)
    a_q, a_kvw = a_heads * hd, a_kv * hd
    b_qk, b_v = b_heads * 2 * hd, b_heads * b_vdim
    t = _tiles(m, seq, d_ff)
    tk = t["tk"]
    nc = seq // tk
    q_scale = (hd ** -0.5) * LOG2E
    tab_a, tab_b = _rope_tables(seq)

    x = x.reshape(m, d)
    for l in range(depth):
        lambda_init = 0.8 - 0.6 * math.exp(-0.3 * l)
        row = lambda v: v[l].reshape(1, -1).astype(F32)
        w_gu1, w_d1 = ffn1_w_gu[l].astype(BF16), ffn1_w_down[l].astype(BF16)
        w_gu2, w_d2 = ffn2_w_gu[l].astype(BF16), ffn2_w_down[l].astype(BF16)
        w_o = w_out[l].astype(BF16)
        c1 = a_q + a_kvw
        c2 = c1 + a_kvw
        c3 = c2 + 2 * b_qk
        w_i = w_in[l]
        w_a = jnp.concatenate([_permute_heads(w_i[:, :c1], AXIAL_ORDER), w_i[:, c1:c2]], axis=1).astype(BF16)
        w_b = jnp.concatenate([_permute_heads(w_i[:, c2:c3], PARTIAL_ORDER), w_i[:, c3:]], axis=1).astype(BF16)

        x, h = _ffn(x, row(ffn1_norm), w_gu1, w_d1, row(mix_norm),
                    emit_norm=True, final_norm=False, tm=t["ffn1_tm"], tf=t["ffn_tf"])

        tqa, tqb, ptm = t["a_tq"], t["b_tq"], t["proj_tm"]
        gain_a = jnp.concatenate([jnp.tile(_permute_heads(row(a_q_norm), AXIAL_ORDER) * q_scale, (1, a_heads)),
                                  jnp.tile(_permute_heads(row(a_k_norm), AXIAL_ORDER), (1, a_kv))], axis=-1)
        qa_t, ka, va_t = _proj(h, w_a, gain_a, *tab_a, bsz=bsz, seq=seq, n_qc=a_heads, n_kc=a_kv,
                               per_group=A_GROUP, dv=hd, tq=tqa, tk=tk, tm=ptm)
        gain_b = jnp.concatenate([jnp.tile(_permute_heads(row(b_q_norm), PARTIAL_ORDER) * q_scale, (1, 2 * b_heads)),
                                  jnp.tile(_permute_heads(row(b_k_norm), PARTIAL_ORDER), (1, 2 * b_heads))], axis=-1)
        qb_t, kb, vb_t = _proj(h, w_b, gain_b, *tab_b, bsz=bsz, seq=seq, n_qc=2 * b_heads, n_kc=2 * b_heads,
                               per_group=2, dv=b_vdim, tq=tqb, tk=tk, tm=ptm)
        ka = ka.reshape(bsz, a_kv, 1, nc, tk, hd)
        kb = kb.reshape(bsz, b_heads, 2, nc, tk, hd)

        out_a = _attn(qa_t, ka, va_t, tq=tqa)
        out_b = _attn(qb_t, kb, vb_t, tq=tqb,
                      diff_args=(row(b_lambda_q1), row(b_lambda_k1), row(b_lambda_q2), row(b_lambda_k2), row(b_subln)),
                      lambda_init=lambda_init)

        x = _outproj(x, out_a.reshape(m, a_q), out_b.reshape(m, b_v), w_o, tm=t["out_tm"])
        x, = _ffn(x, row(ffn2_norm), w_gu2, w_d2, row(out_norm),
                  emit_norm=False, final_norm=True, tm=t["ffn2_tm"], tf=t["ffn_tf"])
    return x.reshape(bsz, seq, d)
```

```python
import functools
import math

import jax
import jax.numpy as jnp
from jax import lax
from jax.experimental import pallas as pl
from jax.experimental.pallas import tpu as pltpu

HEAD_DIM = 128
GRID_W = 64
EPS = 1e-6
A_ROPE_THETA = 10000.0
PARTIAL_ROPE_THETA = 500000.0
AXIAL_DIM = HEAD_DIM // 2
PARTIAL_ROPE_DIM = HEAD_DIM // 4
A_GROUP = 4
LANES = 128
LOG2E = math.log2(math.e)
VMEM_LIMIT_BYTES = 63 * 1024 * 1024
NEG_BIG = -1e30
L_MIN_SAFE = 1e-20

BF16 = jnp.bfloat16
F32 = jnp.float32


def _dot(a, b):
    return jnp.dot(a, b, preferred_element_type=F32)


def _rms(x, gain):
    r = lax.rsqrt(jnp.mean(x * x, axis=-1, keepdims=True) + EPS)
    return (x * r) * gain


def _params(*semantics):
    return pltpu.CompilerParams(dimension_semantics=semantics, vmem_limit_bytes=VMEM_LIMIT_BYTES)


def _ffn_kernel(x_ref, gain_ref, wg_ref, wu_ref, wd_ref, ngain_ref, o_ref, *rest, emit_norm, final_norm):
    if emit_norm:
        hn_ref, xn_ref = rest
    else:
        (xn_ref,) = rest
    j = pl.program_id(1)

    @pl.when(j == 0)
    def _():
        x = x_ref[...]
        xn_ref[...] = _rms(x, gain_ref[...]).astype(BF16)
        o_ref[...] = x

    xn = xn_ref[...]
    g = _dot(xn, wg_ref[...])
    u = _dot(xn, wu_ref[...])
    act = (0.5 * g / (1.0 + jnp.exp(-g))) * u
    o_ref[...] += _dot(act.astype(BF16), wd_ref[...])

    @pl.when(j == pl.num_programs(1) - 1)
    def _():
        y = _rms(o_ref[...], ngain_ref[...])
        if emit_norm:
            hn_ref[...] = y.astype(BF16)
        if final_norm:
            o_ref[...] = y


def _ffn(x, gain, w_gu, w_down, next_gain, *, emit_norm, final_norm, tm, tf):
    m, d = x.shape
    d_ff = w_down.shape[0]
    nf = d_ff // tf
    assert m % tm == 0 and d_ff % tf == 0
    out_shape = [jax.ShapeDtypeStruct((m, d), F32)]
    out_specs = [pl.BlockSpec((tm, d), lambda i, j: (i, 0))]
    if emit_norm:
        out_shape.append(jax.ShapeDtypeStruct((m, d), BF16))
        out_specs.append(pl.BlockSpec((tm, d), lambda i, j: (i, 0)))
    res = pl.pallas_call(
        functools.partial(_ffn_kernel, emit_norm=emit_norm, final_norm=final_norm),
        grid=(m // tm, nf),
        in_specs=[
            pl.BlockSpec((tm, d), lambda i, j: (i, 0)),
            pl.BlockSpec((1, d), lambda i, j: (0, 0)),
            pl.BlockSpec((d, tf), lambda i, j: (0, j)),
            pl.BlockSpec((d, tf), lambda i, j: (0, j + nf)),
            pl.BlockSpec((tf, d), lambda i, j: (j, 0)),
            pl.BlockSpec((1, d), lambda i, j: (0, 0)),
        ],
        out_specs=out_specs,
        out_shape=out_shape,
        scratch_shapes=[pltpu.VMEM((tm, d), BF16)],
        compiler_params=_params("parallel", "arbitrary"),
        name="ffn",
    )(x, gain, w_gu, w_gu, w_down, next_gain)
    return res


PROJ_COLS = 2 * LANES


def _proj_kernel(h_ref, w_ref, gain_ref, cos_ref, sin_ref, q_ref, k_ref, v_ref, acc_a, acc_b, *,
                 n_qc, n_kc, per_group, tq):
    i = pl.program_id(0)
    tm = h_ref.shape[0]
    width = w_ref.shape[1]
    dv = v_ref.shape[1]
    v0 = (n_qc + n_kc) * LANES

    @pl.when(i == 0)
    def _():
        acc_b[...] = jnp.zeros(acc_b.shape, F32)

    def step(mm_ref, ep_ref):
        for c0 in range(0, width, PROJ_COLS):
            cols = min(PROJ_COLS, width - c0)
            mm_ref[:, c0:c0 + cols] = _dot(h_ref[...], w_ref[:, c0:c0 + cols])
        cos, sin = cos_ref[...], sin_ref[...]
        for c in range(n_qc + n_kc):
            sl = slice(c * LANES, (c + 1) * LANES)
            y = _rms(ep_ref[:, sl], gain_ref[:, sl])
            y = y * cos + pltpu.roll(y, LANES // 2, 1) * sin
            if c < n_qc:
                g, r = divmod(c, per_group)
                yt = y.T.astype(BF16)
                for j in range(tm // tq):
                    q_ref[g, j, :, r * tq:(r + 1) * tq] = yt[:, j * tq:(j + 1) * tq]
            else:
                k_ref[c - n_qc] = y.astype(BF16)
        for g in range((width - v0) // dv):
            v_ref[g] = ep_ref[:, v0 + g * dv:v0 + (g + 1) * dv].T.astype(BF16)

    @pl.when(i % 2 == 0)
    def _():
        step(acc_a, acc_b)

    @pl.when(i % 2 == 1)
    def _():
        step(acc_b, acc_a)


def _proj(h, w, gain, cos, sin, *, bsz, seq, n_qc, n_kc, per_group, dv, tq, tk, tm):
    m, d = h.shape
    width = w.shape[1]
    qk_w = (n_qc + n_kc) * LANES
    groups, v_groups = n_qc // per_group, (width - qk_w) // dv
    nt, per_chunk, n = seq // tm, tk // tm, m // tm
    assert tm % tq == 0 and tk % tm == 0 and seq % tk == 0 and (width - qk_w) % dv == 0

    def cur(i):
        return jnp.minimum(i, n - 1)

    def prev(i):
        return jnp.maximum(i - 1, 0)

    return pl.pallas_call(
        functools.partial(_proj_kernel, n_qc=n_qc, n_kc=n_kc, per_group=per_group, tq=tq),
        grid=(n + 1,),
        in_specs=[
            pl.BlockSpec((tm, d), lambda i: (cur(i), 0)),
            pl.BlockSpec((d, width), lambda i: (0, 0)),
            pl.BlockSpec((1, qk_w), lambda i: (0, 0)),
            pl.BlockSpec((tm, LANES), lambda i: (prev(i) % nt, 0)),
            pl.BlockSpec((tm, LANES), lambda i: (prev(i) % nt, 0)),
        ],
        out_specs=[
            pl.BlockSpec((None, groups, tm // tq, LANES, per_group * tq),
                         lambda i: (prev(i) // nt, 0, prev(i) % nt, 0, 0)),
            pl.BlockSpec((None, n_kc, tm, LANES), lambda i: (prev(i) // nt, 0, prev(i) % nt, 0)),
            pl.BlockSpec((None, v_groups, None, dv, tm),
                         lambda i: (prev(i) // nt, 0, (prev(i) % nt) // per_chunk, 0, (prev(i) % nt) % per_chunk)),
        ],
        out_shape=[
            jax.ShapeDtypeStruct((bsz, groups, seq // tq, LANES, per_group * tq), BF16),
            jax.ShapeDtypeStruct((bsz, n_kc, seq, LANES), BF16),
            jax.ShapeDtypeStruct((bsz, v_groups, seq // tk, dv, tk), BF16),
        ],
        scratch_shapes=[pltpu.VMEM((tm, width), F32), pltpu.VMEM((tm, width), F32)],
        compiler_params=_params("arbitrary"),
        name="proj",
    )(h, w, gain, cos, sin)


def _attn_kernel(*refs, tq, diff, lambda_init):
    if diff:
        q_ref, k_ref, v_ref, lq1_ref, lk1_ref, lq2_ref, lk2_ref, subln_ref, o_ref, acc_ref, m_ref, l_ref, l8_ref, kmax_ref = refs
    else:
        q_ref, k_ref, v_ref, o_ref, acc_ref, m_ref, l_ref, l8_ref, kmax_ref = refs
    n_k, n_chunks, tk = k_ref.shape[0], k_ref.shape[1], k_ref.shape[2]
    w = q_ref.shape[1]
    wk = w // n_k

    def scores(c):
        parts = [_dot(k_ref[ks, c], q_ref[:, ks * wk:(ks + 1) * wk]) for ks in range(n_k)]
        return parts[0] if n_k == 1 else jnp.concatenate(parts, axis=1)

    @pl.when(pl.program_id(2) == 0)
    def _():
        for ks in range(n_k):
            def kbody(c, best):
                kf = k_ref[ks, c].astype(F32)
                return jnp.maximum(best, jnp.max(jnp.sum(kf * kf, axis=-1, keepdims=True), axis=0, keepdims=True))
            k2 = lax.fori_loop(0, n_chunks, kbody, jnp.zeros((1, 1), F32))
            kmax_ref[ks] = jnp.broadcast_to(jnp.sqrt(k2), (1, LANES))

    qf = q_ref[...].astype(F32)
    qn = jnp.sqrt(jnp.sum(qf * qf, axis=0, keepdims=True))
    for ks in range(n_k):
        m_ref[:, ks * wk:(ks + 1) * wk] = qn[:, ks * wk:(ks + 1) * wk] * kmax_ref[ks, :, 0:1]
    acc_ref[...] = jnp.zeros(acc_ref.shape, F32)
    l8_ref[...] = jnp.zeros(l8_ref.shape, F32)

    def fast_body(c, carry):
        p = jnp.exp2(scores(c) - m_ref[...])
        l8_ref[...] += jnp.sum(p.reshape(tk // 8, 8, w), axis=0)
        acc_ref[...] += _dot(v_ref[c], p.astype(BF16))
        return carry

    lax.fori_loop(0, n_chunks, fast_body, 0)
    l_ref[...] = jnp.sum(l8_ref[...], axis=0, keepdims=True)
    underflow = jnp.min(l_ref[...]) < L_MIN_SAFE

    @pl.when(underflow)
    def _():
        acc_ref[...] = jnp.zeros(acc_ref.shape, F32)
        m_ref[...] = jnp.full(m_ref.shape, NEG_BIG, F32)
        l_ref[...] = jnp.zeros(l_ref.shape, F32)

        def body(c, carry):
            s = scores(c)
            m_old = m_ref[...]
            m_new = jnp.maximum(m_old, jnp.max(s, axis=0, keepdims=True))
            alpha = jnp.exp2(m_old - m_new)
            p = jnp.exp2(s - m_new)
            l_ref[...] = alpha * l_ref[...] + jnp.sum(p, axis=0, keepdims=True)
            acc_ref[...] = alpha * acc_ref[...] + _dot(v_ref[c], p.astype(BF16))
            m_ref[...] = m_new
            return carry

        lax.fori_loop(0, n_chunks, body, 0)

    o = acc_ref[...] / l_ref[...]
    if not diff:
        for r in range(w // tq):
            o_ref[:, r * LANES:(r + 1) * LANES] = o[:, r * tq:(r + 1) * tq].T.astype(BF16)
    else:
        lam = (jnp.exp(jnp.sum(lq1_ref[...] * lk1_ref[...], axis=-1, keepdims=True))
               - jnp.exp(jnp.sum(lq2_ref[...] * lk2_ref[...], axis=-1, keepdims=True))
               + lambda_init)
        d = o[:, :tq] - lam * o[:, tq:]
        y = _rms(d.T, subln_ref[...]) * (1.0 - lambda_init)
        o_ref[...] = y.astype(BF16)


def _attn(qt, k, vt, *, tq, diff_args=None, lambda_init=0.0):
    b, g, nqt, hd, w = qt.shape
    n_k, nc, tk = k.shape[2], k.shape[3], k.shape[4]
    dv = vt.shape[3]
    diff = diff_args is not None
    width = dv if diff else (w // tq) * dv
    in_specs = [
        pl.BlockSpec((None, None, None, hd, w), lambda bi, gi, qi: (bi, gi, qi, 0, 0)),
        pl.BlockSpec((None, None, n_k, nc, tk, hd), lambda bi, gi, qi: (bi, gi, 0, 0, 0, 0)),
        pl.BlockSpec((None, None, nc, dv, tk), lambda bi, gi, qi: (bi, gi, 0, 0, 0)),
    ]
    args = [qt, k, vt]
    if diff:
        in_specs += [pl.BlockSpec((1, hd), lambda bi, gi, qi: (0, 0))] * 4
        in_specs += [pl.BlockSpec((1, dv), lambda bi, gi, qi: (0, 0))]
        args += list(diff_args)
    return pl.pallas_call(
        functools.partial(_attn_kernel, tq=tq, diff=diff, lambda_init=lambda_init),
        grid=(b, g, nqt),
        in_specs=in_specs,
        out_specs=pl.BlockSpec((None, tq, width), lambda bi, gi, qi: (bi, qi, gi)),
        out_shape=jax.ShapeDtypeStruct((b, nqt * tq, g * width), BF16),
        scratch_shapes=[
            pltpu.VMEM((dv, w), F32),
            pltpu.VMEM((1, w), F32),
            pltpu.VMEM((1, w), F32),
            pltpu.VMEM((8, w), F32),
            pltpu.VMEM((n_k, 1, LANES), F32),
        ],
        compiler_params=_params("parallel", "parallel", "arbitrary"),
        name="attn_diff" if diff else "attn_gqa",
    )(*args)


def _outproj_kernel(x_ref, a_ref, b_ref, wa_ref, wb_ref, o_ref):
    o_ref[...] = x_ref[...] + _dot(a_ref[...], wa_ref[...]) + _dot(b_ref[...], wb_ref[...])


def _outproj(x, a, b, w_out, *, tm):
    m, d = x.shape
    da, db = a.shape[1], b.shape[1]
    assert da == db and w_out.shape[0] == da + db
    return pl.pallas_call(
        _outproj_kernel,
        grid=(m // tm,),
        in_specs=[
            pl.BlockSpec((tm, d), lambda i: (i, 0)),
            pl.BlockSpec((tm, da), lambda i: (i, 0)),
            pl.BlockSpec((tm, db), lambda i: (i, 0)),
            pl.BlockSpec((da, d), lambda i: (0, 0)),
            pl.BlockSpec((db, d), lambda i: (1, 0)),
        ],
        out_specs=pl.BlockSpec((tm, d), lambda i: (i, 0)),
        out_shape=jax.ShapeDtypeStruct((m, d), F32),
        compiler_params=_params("parallel"),
        name="outproj",
    )(x, a, b, w_out, w_out)


def _rope_cs(pos, dim, theta):
    inv_freq = theta ** (-jnp.arange(0, dim, 2, dtype=F32) / dim)
    ang = pos[:, None] * inv_freq[None, :]
    return jnp.cos(ang), jnp.sin(ang)


ROPE_BLOCK = 16
AXIAL_ORDER = (0, 1, 4, 5, 2, 3, 6, 7)
PARTIAL_ORDER = (0, 2, 3, 4, 1, 5, 6, 7)


def _permute_heads(a, order):
    lead = a.shape[:-1]
    a4 = a.reshape(lead + (a.shape[-1] // HEAD_DIM, HEAD_DIM // ROPE_BLOCK, ROPE_BLOCK))
    return jnp.concatenate([a4[..., i:i + 1, :] for i in order], axis=-2).reshape(a.shape)


def _rope_tables(seq):
    rows = seq // GRID_W
    rc, rs = (jnp.repeat(a, GRID_W, axis=0) for a in _rope_cs(jnp.arange(rows, dtype=F32), AXIAL_DIM, A_ROPE_THETA))
    cc, cs = (jnp.tile(a, (rows, 1)) for a in _rope_cs(jnp.arange(GRID_W, dtype=F32), AXIAL_DIM, A_ROPE_THETA))
    pc, ps = _rope_cs(jnp.arange(seq, dtype=F32), PARTIAL_ROPE_DIM, PARTIAL_ROPE_THETA)
    cos_a = jnp.concatenate([rc, cc, rc, cc], axis=-1)
    sin_a = jnp.concatenate([-rs, -cs, rs, cs], axis=-1)
    rest = HEAD_DIM // 2 - PARTIAL_ROPE_DIM // 2
    one, zero = jnp.ones((seq, rest), F32), jnp.zeros((seq, rest), F32)
    cos_b = jnp.concatenate([pc, one, pc, one], axis=-1)
    sin_b = jnp.concatenate([-ps, zero, ps, zero], axis=-1)
    return (cos_a, sin_a), (cos_b, sin_b)


def _tiles(m, seq, d_ff):
    def fit(n, t):
        t = min(t, n)
        while n % t:
            t //= 2
        return t
    return dict(
        ffn1_tm=fit(m, 1024), ffn2_tm=fit(m, 1024), ffn_tf=fit(d_ff, 512),
        proj_tm=fit(seq, 512), out_tm=fit(m, 512),
        a_tq=fit(seq, 256), b_tq=fit(seq, 512), tk=fit(seq, 4096),
    )


def kernel(x, ffn1_norm, ffn1_w_gu, ffn1_w_down, mix_norm, w_in, a_q_norm, a_k_norm, b_q_norm, b_k_norm,
           b_lambda_q1, b_lambda_k1, b_lambda_q2, b_lambda_k2, b_subln, w_out, ffn2_norm, ffn2_w_gu,
           ffn2_w_down, out_norm):
    bsz, seq, d = x.shape
    depth = w_in.shape[0]
    d_ff = ffn1_w_down.shape[1]
    m = bsz * seq
    hd = HEAD_DIM
    a_heads = d // (2 * hd)
    a_kv = a_heads // A_GROUP
    b_vdim = 2 * hd
    b_heads = d // (2 * b_vdim)
    a_q, a_kvw = a_heads * hd, a_kv * hd
    b_qk, b_v = b_heads * 2 * hd, b_heads * b_vdim
    t = _tiles(m, seq, d_ff)
    tk = t["tk"]
    nc = seq // tk
    q_scale = (hd ** -0.5) * LOG2E
    tab_a, tab_b = _rope_tables(seq)

    x = x.reshape(m, d)
    for l in range(depth):
        lambda_init = 0.8 - 0.6 * math.exp(-0.3 * l)
        row = lambda v: v[l].reshape(1, -1).astype(F32)
        w_gu1, w_d1 = ffn1_w_gu[l].astype(BF16), ffn1_w_down[l].astype(BF16)
        w_gu2, w_d2 = ffn2_w_gu[l].astype(BF16), ffn2_w_down[l].astype(BF16)
        w_o = w_out[l].astype(BF16)
        c1 = a_q + a_kvw
        c2 = c1 + a_kvw
        c3 = c2 + 2 * b_qk
        w_i = w_in[l]
        w_a = jnp.concatenate([_permute_heads(w_i[:, :c1], AXIAL_ORDER), w_i[:, c1:c2]], axis=1).astype(BF16)
        w_b = jnp.concatenate([_permute_heads(w_i[:, c2:c3], PARTIAL_ORDER), w_i[:, c3:]], axis=1).astype(BF16)

        x, h = _ffn(x, row(ffn1_norm), w_gu1, w_d1, row(mix_norm),
                    emit_norm=True, final_norm=False, tm=t["ffn1_tm"], tf=t["ffn_tf"])

        tqa, tqb, ptm = t["a_tq"], t["b_tq"], t["proj_tm"]
        gain_a = jnp.concatenate([jnp.tile(_permute_heads(row(a_q_norm), AXIAL_ORDER) * q_scale, (1, a_heads)),
                                  jnp.tile(_permute_heads(row(a_k_norm), AXIAL_ORDER), (1, a_kv))], axis=-1)
        qa_t, ka, va_t = _proj(h, w_a, gain_a, *tab_a, bsz=bsz, seq=seq, n_qc=a_heads, n_kc=a_kv,
                               per_group=A_GROUP, dv=hd, tq=tqa, tk=tk, tm=ptm)
        gain_b = jnp.concatenate([jnp.tile(_permute_heads(row(b_q_norm), PARTIAL_ORDER) * q_scale, (1, 2 * b_heads)),
                                  jnp.tile(_permute_heads(row(b_k_norm), PARTIAL_ORDER), (1, 2 * b_heads))], axis=-1)
        qb_t, kb, vb_t = _proj(h, w_b, gain_b, *tab_b, bsz=bsz, seq=seq, n_qc=2 * b_heads, n_kc=2 * b_heads,
                               per_group=2, dv=b_vdim, tq=tqb, tk=tk, tm=ptm)
        ka = ka.reshape(bsz, a_kv, 1, nc, tk, hd)
        kb = kb.reshape(bsz, b_heads, 2, nc, tk, hd)

        out_a = _attn(qa_t, ka, va_t, tq=tqa)
        out_b = _attn(qb_t, kb, vb_t, tq=tqb,
                      diff_args=(row(b_lambda_q1), row(b_lambda_k1), row(b_lambda_q2), row(b_lambda_k2), row(b_subln)),
                      lambda_init=lambda_init)

        x = _outproj(x, out_a.reshape(m, a_q), out_b.reshape(m, b_v), w_o, tm=t["out_tm"])
        x, = _ffn(x, row(ffn2_norm), w_gu2, w_d2, row(out_norm),
                  emit_norm=False, final_norm=True, tm=t["ffn2_tm"], tf=t["ffn_tf"])
    return x.reshape(bsz, seq, d)
```

```python
import functools
import math

import jax
import jax.numpy as jnp
from jax import lax
from jax.experimental import pallas as pl
from jax.experimental.pallas import tpu as pltpu

HEAD_DIM = 128
GRID_W = 64
EPS = 1e-6
A_ROPE_THETA = 10000.0
PARTIAL_ROPE_THETA = 500000.0
AXIAL_DIM = HEAD_DIM // 2
PARTIAL_ROPE_DIM = HEAD_DIM // 4
A_GROUP = 4
LANES = 128
LOG2E = math.log2(math.e)
VMEM_LIMIT_BYTES = 63 * 1024 * 1024
NEG_BIG = -1e30
L_MIN_SAFE = 1e-20

BF16 = jnp.bfloat16
F32 = jnp.float32


def _dot(a, b):
    return jnp.dot(a, b, preferred_element_type=F32)


def _rms(x, gain):
    r = lax.rsqrt(jnp.mean(x * x, axis=-1, keepdims=True) + EPS)
    return (x * r) * gain


def _params(*semantics):
    return pltpu.CompilerParams(dimension_semantics=semantics, vmem_limit_bytes=VMEM_LIMIT_BYTES)


def _ffn_kernel(x_ref, gain_ref, wg_ref, wu_ref, wd_ref, ngain_ref, o_ref, *rest, emit_norm, final_norm):
    if emit_norm:
        hn_ref, xn_ref = rest
    else:
        (xn_ref,) = rest
    j = pl.program_id(1)

    @pl.when(j == 0)
    def _():
        x = x_ref[...]
        xn_ref[...] = _rms(x, gain_ref[...]).astype(BF16)
        o_ref[...] = x

    xn = xn_ref[...]
    g = _dot(xn, wg_ref[...])
    u = _dot(xn, wu_ref[...])
    act = (0.5 * g / (1.0 + jnp.exp(-g))) * u
    o_ref[...] += _dot(act.astype(BF16), wd_ref[...])

    @pl.when(j == pl.num_programs(1) - 1)
    def _():
        y = _rms(o_ref[...], ngain_ref[...])
        if emit_norm:
            hn_ref[...] = y.astype(BF16)
        if final_norm:
            o_ref[...] = y


def _ffn(x, gain, w_gu, w_down, next_gain, *, emit_norm, final_norm, tm, tf):
    m, d = x.shape
    d_ff = w_down.shape[0]
    nf = d_ff // tf
    assert m % tm == 0 and d_ff % tf == 0
    out_shape = [jax.ShapeDtypeStruct((m, d), F32)]
    out_specs = [pl.BlockSpec((tm, d), lambda i, j: (i, 0))]
    if emit_norm:
        out_shape.append(jax.ShapeDtypeStruct((m, d), BF16))
        out_specs.append(pl.BlockSpec((tm, d), lambda i, j: (i, 0)))
    res = pl.pallas_call(
        functools.partial(_ffn_kernel, emit_norm=emit_norm, final_norm=final_norm),
        grid=(m // tm, nf),
        in_specs=[
            pl.BlockSpec((tm, d), lambda i, j: (i, 0)),
            pl.BlockSpec((1, d), lambda i, j: (0, 0)),
            pl.BlockSpec((d, tf), lambda i, j: (0, j)),
            pl.BlockSpec((d, tf), lambda i, j: (0, j + nf)),
            pl.BlockSpec((tf, d), lambda i, j: (j, 0)),
            pl.BlockSpec((1, d), lambda i, j: (0, 0)),
        ],
        out_specs=out_specs,
        out_shape=out_shape,
        scratch_shapes=[pltpu.VMEM((tm, d), BF16)],
        compiler_params=_params("parallel", "arbitrary"),
        name="ffn",
    )(x, gain, w_gu, w_gu, w_down, next_gain)
    return res


PROJ_COLS = 2 * LANES


def _proj_kernel(h_ref, w_ref, gain_ref, cos_ref, sin_ref, q_ref, k_ref, v_ref, acc_a, acc_b, *,
                 n_qc, n_kc, per_group, tq):
    i = pl.program_id(0)
    tm = h_ref.shape[0]
    width = w_ref.shape[1]
    dv = v_ref.shape[1]
    v0 = (n_qc + n_kc) * LANES

    @pl.when(i == 0)
    def _():
        acc_b[...] = jnp.zeros(acc_b.shape, F32)

    def step(mm_ref, ep_ref):
        for c0 in range(0, width, PROJ_COLS):
            cols = min(PROJ_COLS, width - c0)
            mm_ref[:, c0:c0 + cols] = _dot(h_ref[...], w_ref[:, c0:c0 + cols])
        cos, sin = cos_ref[...], sin_ref[...]
        for c in range(n_qc + n_kc):
            sl = slice(c * LANES, (c + 1) * LANES)
            y = _rms(ep_ref[:, sl], gain_ref[:, sl])
            y = y * cos + pltpu.roll(y, LANES // 2, 1) * sin
            if c < n_qc:
                g, r = divmod(c, per_group)
                yt = y.T.astype(BF16)
                for j in range(tm // tq):
                    q_ref[g, j, :, r * tq:(r + 1) * tq] = yt[:, j * tq:(j + 1) * tq]
            else:
                k_ref[c - n_qc] = y.astype(BF16)
        for g in range((width - v0) // dv):
            v_ref[g] = ep_ref[:, v0 + g * dv:v0 + (g + 1) * dv].T.astype(BF16)

    @pl.when(i % 2 == 0)
    def _():
        step(acc_a, acc_b)

    @pl.when(i % 2 == 1)
    def _():
        step(acc_b, acc_a)


def _proj(h, w, gain, cos, sin, *, bsz, seq, n_qc, n_kc, per_group, dv, tq, tk, tm):
    m, d = h.shape
    width = w.shape[1]
    qk_w = (n_qc + n_kc) * LANES
    groups, v_groups = n_qc // per_group, (width - qk_w) // dv
    nt, per_chunk, n = seq // tm, tk // tm, m // tm
    assert tm % tq == 0 and tk % tm == 0 and seq % tk == 0 and (width - qk_w) % dv == 0

    def cur(i):
        return jnp.minimum(i, n - 1)

    def prev(i):
        return jnp.maximum(i - 1, 0)

    return pl.pallas_call(
        functools.partial(_proj_kernel, n_qc=n_qc, n_kc=n_kc, per_group=per_group, tq=tq),
        grid=(n + 1,),
        in_specs=[
            pl.BlockSpec((tm, d), lambda i: (cur(i), 0)),
            pl.BlockSpec((d, width), lambda i: (0, 0)),
            pl.BlockSpec((1, qk_w), lambda i: (0, 0)),
            pl.BlockSpec((tm, LANES), lambda i: (prev(i) % nt, 0)),
            pl.BlockSpec((tm, LANES), lambda i: (prev(i) % nt, 0)),
        ],
        out_specs=[
            pl.BlockSpec((None, groups, tm // tq, LANES, per_group * tq),
                         lambda i: (prev(i) // nt, 0, prev(i) % nt, 0, 0)),
            pl.BlockSpec((None, n_kc, tm, LANES), lambda i: (prev(i) // nt, 0, prev(i) % nt, 0)),
            pl.BlockSpec((None, v_groups, None, dv, tm),
                         lambda i: (prev(i) // nt, 0, (prev(i) % nt) // per_chunk, 0, (prev(i) % nt) % per_chunk)),
        ],
        out_shape=[
            jax.ShapeDtypeStruct((bsz, groups, seq // tq, LANES, per_group * tq), BF16),
            jax.ShapeDtypeStruct((bsz, n_kc, seq, LANES), BF16),
            jax.ShapeDtypeStruct((bsz, v_groups, seq // tk, dv, tk), BF16),
        ],
        scratch_shapes=[pltpu.VMEM((tm, width), F32), pltpu.VMEM((tm, width), F32)],
        compiler_params=_params("arbitrary"),
        name="proj",
    )(h, w, gain, cos, sin)


def _attn_kernel(*refs, tq, diff, lambda_init):
    if diff:
        q_ref, k_ref, v_ref, lq1_ref, lk1_ref, lq2_ref, lk2_ref, subln_ref, o_ref, acc_ref, m_ref, l_ref, l8_ref, kmax_ref = refs
    else:
        q_ref, k_ref, v_ref, o_ref, acc_ref, m_ref, l_ref, l8_ref, kmax_ref = refs
    n_k, n_chunks, tk = k_ref.shape[0], k_ref.shape[1], k_ref.shape[2]
    w = q_ref.shape[1]
    wk = w // n_k

    def scores(c):
        parts = [_dot(k_ref[ks, c], q_ref[:, ks * wk:(ks + 1) * wk]) for ks in range(n_k)]
        return parts[0] if n_k == 1 else jnp.concatenate(parts, axis=1)

    @pl.when(pl.program_id(2) == 0)
    def _():
        for ks in range(n_k):
            def kbody(c, best):
                kf = k_ref[ks, c].astype(F32)
                return jnp.maximum(best, jnp.max(jnp.sum(kf * kf, axis=-1, keepdims=True), axis=0, keepdims=True))
            k2 = lax.fori_loop(0, n_chunks, kbody, jnp.zeros((1, 1), F32))
            kmax_ref[ks] = jnp.broadcast_to(jnp.sqrt(k2), (1, LANES))

    qf = q_ref[...].astype(F32)
    qn = jnp.sqrt(jnp.sum(qf * qf, axis=0, keepdims=True))
    for ks in range(n_k):
        m_ref[:, ks * wk:(ks + 1) * wk] = qn[:, ks * wk:(ks + 1) * wk] * kmax_ref[ks, :, 0:1]
    acc_ref[...] = jnp.zeros(acc_ref.shape, F32)
    l8_ref[...] = jnp.zeros(l8_ref.shape, F32)

    def fast_body(c, carry):
        p = jnp.exp2(scores(c) - m_ref[...])
        l8_ref[...] += jnp.sum(p.reshape(tk // 8, 8, w), axis=0)
        acc_ref[...] += _dot(v_ref[c], p.astype(BF16))
        return carry

    lax.fori_loop(0, n_chunks, fast_body, 0)
    l_ref[...] = jnp.sum(l8_ref[...], axis=0, keepdims=True)
    underflow = jnp.min(l_ref[...]) < L_MIN_SAFE

    @pl.when(underflow)
    def _():
        acc_ref[...] = jnp.zeros(acc_ref.shape, F32)
        m_ref[...] = jnp.full(m_ref.shape, NEG_BIG, F32)
        l_ref[...] = jnp.zeros(l_ref.shape, F32)

        def body(c, carry):
            s = scores(c)
            m_old = m_ref[...]
            m_new = jnp.maximum(m_old, jnp.max(s, axis=0, keepdims=True))
            alpha = jnp.exp2(m_old - m_new)
            p = jnp.exp2(s - m_new)
            l_ref[...] = alpha * l_ref[...] + jnp.sum(p, axis=0, keepdims=True)
            acc_ref[...] = alpha * acc_ref[...] + _dot(v_ref[c], p.astype(BF16))
            m_ref[...] = m_new
            return carry

        lax.fori_loop(0, n_chunks, body, 0)

    o = acc_ref[...] * (1.0 / l_ref[...])
    if not diff:
        for r in range(w // tq):
            o_ref[:, r * LANES:(r + 1) * LANES] = o[:, r * tq:(r + 1) * tq].T.astype(BF16)
    else:
        lam = (jnp.exp(jnp.sum(lq1_ref[...] * lk1_ref[...], axis=-1, keepdims=True))
               - jnp.exp(jnp.sum(lq2_ref[...] * lk2_ref[...], axis=-1, keepdims=True))
               + lambda_init)
        d = o[:, :tq] - lam * o[:, tq:]
        y = _rms(d.T, subln_ref[...]) * (1.0 - lambda_init)
        o_ref[...] = y.astype(BF16)


def _attn(qt, k, vt, *, tq, diff_args=None, lambda_init=0.0):
    b, g, nqt, hd, w = qt.shape
    n_k, nc, tk = k.shape[2], k.shape[3], k.shape[4]
    dv = vt.shape[3]
    diff = diff_args is not None
    width = dv if diff else (w // tq) * dv
    in_specs = [
        pl.BlockSpec((None, None, None, hd, w), lambda bi, gi, qi: (bi, gi, qi, 0, 0)),
        pl.BlockSpec((None, None, n_k, nc, tk, hd), lambda bi, gi, qi: (bi, gi, 0, 0, 0, 0)),
        pl.BlockSpec((None, None, nc, dv, tk), lambda bi, gi, qi: (bi, gi, 0, 0, 0)),
    ]
    args = [qt, k, vt]
    if diff:
        in_specs += [pl.BlockSpec((1, hd), lambda bi, gi, qi: (0, 0))] * 4
        in_specs += [pl.BlockSpec((1, dv), lambda bi, gi, qi: (0, 0))]
        args += list(diff_args)
    return pl.pallas_call(
        functools.partial(_attn_kernel, tq=tq, diff=diff, lambda_init=lambda_init),
        grid=(b, g, nqt),
        in_specs=in_specs,
        out_specs=pl.BlockSpec((None, tq, width), lambda bi, gi, qi: (bi, qi, gi)),
        out_shape=jax.ShapeDtypeStruct((b, nqt * tq, g * width), BF16),
        scratch_shapes=[
            pltpu.VMEM((dv, w), F32),
            pltpu.VMEM((1, w), F32),
            pltpu.VMEM((1, w), F32),
            pltpu.VMEM((8, w), F32),
            pltpu.VMEM((n_k, 1, LANES), F32),
        ],
        compiler_params=_params("parallel", "parallel", "arbitrary"),
        name="attn_diff" if diff else "attn_gqa",
    )(*args)


def _outproj_kernel(x_ref, a_ref, b_ref, wa_ref, wb_ref, o_ref):
    o_ref[...] = x_ref[...] + _dot(a_ref[...], wa_ref[...]) + _dot(b_ref[...], wb_ref[...])


def _outproj(x, a, b, w_out, *, tm):
    m, d = x.shape
    da, db = a.shape[1], b.shape[1]
    assert da == db and w_out.shape[0] == da + db
    return pl.pallas_call(
        _outproj_kernel,
        grid=(m // tm,),
        in_specs=[
            pl.BlockSpec((tm, d), lambda i: (i, 0)),
            pl.BlockSpec((tm, da), lambda i: (i, 0)),
            pl.BlockSpec((tm, db), lambda i: (i, 0)),
            pl.BlockSpec((da, d), lambda i: (0, 0)),
            pl.BlockSpec((db, d), lambda i: (1, 0)),
        ],
        out_specs=pl.BlockSpec((tm, d), lambda i: (i, 0)),
        out_shape=jax.ShapeDtypeStruct((m, d), F32),
        compiler_params=_params("parallel"),
        name="outproj",
    )(x, a, b, w_out, w_out)


def _rope_cs(pos, dim, theta):
    inv_freq = theta ** (-jnp.arange(0, dim, 2, dtype=F32) / dim)
    ang = pos[:, None] * inv_freq[None, :]
    return jnp.cos(ang), jnp.sin(ang)


ROPE_BLOCK = 16
AXIAL_ORDER = (0, 1, 4, 5, 2, 3, 6, 7)
PARTIAL_ORDER = (0, 2, 3, 4, 1, 5, 6, 7)


def _permute_heads(a, order):
    lead = a.shape[:-1]
    a4 = a.reshape(lead + (a.shape[-1] // HEAD_DIM, HEAD_DIM // ROPE_BLOCK, ROPE_BLOCK))
    return jnp.concatenate([a4[..., i:i + 1, :] for i in order], axis=-2).reshape(a.shape)


def _rope_tables(seq):
    rows = seq // GRID_W
    rc, rs = (jnp.repeat(a, GRID_W, axis=0) for a in _rope_cs(jnp.arange(rows, dtype=F32), AXIAL_DIM, A_ROPE_THETA))
    cc, cs = (jnp.tile(a, (rows, 1)) for a in _rope_cs(jnp.arange(GRID_W, dtype=F32), AXIAL_DIM, A_ROPE_THETA))
    pc, ps = _rope_cs(jnp.arange(seq, dtype=F32), PARTIAL_ROPE_DIM, PARTIAL_ROPE_THETA)
    cos_a = jnp.concatenate([rc, cc, rc, cc], axis=-1)
    sin_a = jnp.concatenate([-rs, -cs, rs, cs], axis=-1)
    rest = HEAD_DIM // 2 - PARTIAL_ROPE_DIM // 2
    one, zero = jnp.ones((seq, rest), F32), jnp.zeros((seq, rest), F32)
    cos_b = jnp.concatenate([pc, one, pc, one], axis=-1)
    sin_b = jnp.concatenate([-ps, zero, ps, zero], axis=-1)
    return (cos_a, sin_a), (cos_b, sin_b)


def _tiles(m, seq, d_ff):
    def fit(n, t):
        t = min(t, n)
        while n % t:
            t //= 2
        return t
    return dict(
        ffn1_tm=fit(m, 1024), ffn2_tm=fit(m, 1024), ffn_tf=fit(d_ff, 512),
        proj_tm=fit(seq, 512), out_tm=fit(m, 512),
        a_tq=fit(seq, 256), b_tq=fit(seq, 512), tk=fit(seq, 4096),
    )


def kernel(x, ffn1_norm, ffn1_w_gu, ffn1_w_down, mix_norm, w_in, a_q_norm, a_k_norm, b_q_norm, b_k_norm,
           b_lambda_q1, b_lambda_k1, b_lambda_q2, b_lambda_k2, b_subln, w_out, ffn2_norm, ffn2_w_gu,
           ffn2_w_down, out_norm):
    bsz, seq, d = x.shape
    depth = w_in.shape[0]
    d_ff = ffn1_w_down.shape[1]
    m = bsz * seq
    hd = HEAD_DIM
    a_heads = d // (2 * hd)
    a_kv = a_heads // A_GROUP
    b_vdim = 2 * hd
    b_heads = d // (2 * b_vdim)
    a_q, a_kvw = a_heads * hd, a_kv * hd
    b_qk, b_v = b_heads * 2 * hd, b_heads * b_vdim
    t = _tiles(m, seq, d_ff)
    tk = t["tk"]
    nc = seq // tk
    q_scale = (hd ** -0.5) * LOG2E
    tab_a, tab_b = _rope_tables(seq)

    x = x.reshape(m, d)
    for l in range(depth):
        lambda_init = 0.8 - 0.6 * math.exp(-0.3 * l)
        row = lambda v: v[l].reshape(1, -1).astype(F32)
        w_gu1, w_d1 = ffn1_w_gu[l].astype(BF16), ffn1_w_down[l].astype(BF16)
        w_gu2, w_d2 = ffn2_w_gu[l].astype(BF16), ffn2_w_down[l].astype(BF16)
        w_o = w_out[l].astype(BF16)
        c1 = a_q + a_kvw
        c2 = c1 + a_kvw
        c3 = c2 + 2 * b_qk
        w_i = w_in[l]
        w_a = jnp.concatenate([_permute_heads(w_i[:, :c1], AXIAL_ORDER), w_i[:, c1:c2]], axis=1).astype(BF16)
        w_b = jnp.concatenate([_permute_heads(w_i[:, c2:c3], PARTIAL_ORDER), w_i[:, c3:]], axis=1).astype(BF16)

        x, h = _ffn(x, row(ffn1_norm), w_gu1, w_d1, row(mix_norm),
                    emit_norm=True, final_norm=False, tm=t["ffn1_tm"], tf=t["ffn_tf"])

        tqa, tqb, ptm = t["a_tq"], t["b_tq"], t["proj_tm"]
        gain_a = jnp.concatenate([jnp.tile(_permute_heads(row(a_q_norm), AXIAL_ORDER) * q_scale, (1, a_heads)),
                                  jnp.tile(_permute_heads(row(a_k_norm), AXIAL_ORDER), (1, a_kv))], axis=-1)
        qa_t, ka, va_t = _proj(h, w_a, gain_a, *tab_a, bsz=bsz, seq=seq, n_qc=a_heads, n_kc=a_kv,
                               per_group=A_GROUP, dv=hd, tq=tqa, tk=tk, tm=ptm)
        gain_b = jnp.concatenate([jnp.tile(_permute_heads(row(b_q_norm), PARTIAL_ORDER) * q_scale, (1, 2 * b_heads)),
                                  jnp.tile(_permute_heads(row(b_k_norm), PARTIAL_ORDER), (1, 2 * b_heads))], axis=-1)
        qb_t, kb, vb_t = _proj(h, w_b, gain_b, *tab_b, bsz=bsz, seq=seq, n_qc=2 * b_heads, n_kc=2 * b_heads,
                               per_group=2, dv=b_vdim, tq=tqb, tk=tk, tm=ptm)
        ka = ka.reshape(bsz, a_kv, 1, nc, tk, hd)
        kb = kb.reshape(bsz, b_heads, 2, nc, tk, hd)

        out_a = _attn(qa_t, ka, va_t, tq=tqa)
        out_b = _attn(qb_t, kb, vb_t, tq=tqb,
                      diff_args=(row(b_lambda_q1), row(b_lambda_k1), row(b_lambda_q2), row(b_lambda_k2), row(b_subln)),
                      lambda_init=lambda_init)

        x = _outproj(x, out_a.reshape(m, a_q), out_b.reshape(m, b_v), w_o, tm=t["out_tm"])
        x, = _ffn(x, row(ffn2_norm), w_gu2, w_d2, row(out_norm),
                  emit_norm=False, final_norm=True, tm=t["ffn2_tm"], tf=t["ffn_tf"])
    return x.reshape(bsz, seq, d)
```

```python
import functools
import math

import jax
import jax.numpy as jnp
from jax import lax
from jax.experimental import pallas as pl
from jax.experimental.pallas import tpu as pltpu

HEAD_DIM = 128
GRID_W = 64
EPS = 1e-6
A_ROPE_THETA = 10000.0
PARTIAL_ROPE_THETA = 500000.0
AXIAL_DIM = HEAD_DIM // 2
PARTIAL_ROPE_DIM = HEAD_DIM // 4
A_GROUP = 4
LANES = 128
LOG2E = math.log2(math.e)
VMEM_LIMIT_BYTES = 63 * 1024 * 1024
NEG_BIG = -1e30
L_MIN_SAFE = 1e-20

BF16 = jnp.bfloat16
F32 = jnp.float32


def _dot(a, b):
    return jnp.dot(a, b, preferred_element_type=F32)


def _rms(x, gain):
    r = lax.rsqrt(jnp.mean(x * x, axis=-1, keepdims=True) + EPS)
    return (x * r) * gain


def _params(*semantics):
    return pltpu.CompilerParams(dimension_semantics=semantics, vmem_limit_bytes=VMEM_LIMIT_BYTES)


def _ffn_kernel(x_ref, gain_ref, wg_ref, wu_ref, wd_ref, ngain_ref, o_ref, *rest, emit_norm, final_norm):
    if emit_norm:
        hn_ref, xn_ref = rest
    else:
        (xn_ref,) = rest
    j = pl.program_id(1)

    @pl.when(j == 0)
    def _():
        x = x_ref[...]
        xn_ref[...] = _rms(x, gain_ref[...]).astype(BF16)
        o_ref[...] = x

    xn = xn_ref[...]
    g = _dot(xn, wg_ref[...])
    u = _dot(xn, wu_ref[...])
    act = (0.5 * g / (1.0 + jnp.exp(-g))) * u
    o_ref[...] += _dot(act.astype(BF16), wd_ref[...])

    @pl.when(j == pl.num_programs(1) - 1)
    def _():
        y = _rms(o_ref[...], ngain_ref[...])
        if emit_norm:
            hn_ref[...] = y.astype(BF16)
        if final_norm:
            o_ref[...] = y


def _ffn(x, gain, w_gu, w_down, next_gain, *, emit_norm, final_norm, tm, tf):
    m, d = x.shape
    d_ff = w_down.shape[0]
    nf = d_ff // tf
    assert m % tm == 0 and d_ff % tf == 0
    out_shape = [jax.ShapeDtypeStruct((m, d), F32)]
    out_specs = [pl.BlockSpec((tm, d), lambda i, j: (i, 0))]
    if emit_norm:
        out_shape.append(jax.ShapeDtypeStruct((m, d), BF16))
        out_specs.append(pl.BlockSpec((tm, d), lambda i, j: (i, 0)))
    res = pl.pallas_call(
        functools.partial(_ffn_kernel, emit_norm=emit_norm, final_norm=final_norm),
        grid=(m // tm, nf),
        in_specs=[
            pl.BlockSpec((tm, d), lambda i, j: (i, 0)),
            pl.BlockSpec((1, d), lambda i, j: (0, 0)),
            pl.BlockSpec((d, tf), lambda i, j: (0, j)),
            pl.BlockSpec((d, tf), lambda i, j: (0, j + nf)),
            pl.BlockSpec((tf, d), lambda i, j: (j, 0)),
            pl.BlockSpec((1, d), lambda i, j: (0, 0)),
        ],
        out_specs=out_specs,
        out_shape=out_shape,
        scratch_shapes=[pltpu.VMEM((tm, d), BF16)],
        compiler_params=_params("parallel", "arbitrary"),
        name="ffn",
    )(x, gain, w_gu, w_gu, w_down, next_gain)
    return res


PROJ_COLS = 2 * LANES


def _proj_kernel(h_ref, w_ref, gain_ref, cos_ref, sin_ref, q_ref, k_ref, v_ref, acc_a, acc_b, *,
                 n_qc, n_kc, per_group, tq):
    i = pl.program_id(0)
    tm = h_ref.shape[0]
    width = w_ref.shape[1]
    dv = v_ref.shape[1]
    v0 = (n_qc + n_kc) * LANES

    @pl.when(i == 0)
    def _():
        acc_b[...] = jnp.zeros(acc_b.shape, F32)

    def step(mm_ref, ep_ref):
        for c0 in range(0, width, PROJ_COLS):
            cols = min(PROJ_COLS, width - c0)
            mm_ref[:, c0:c0 + cols] = _dot(h_ref[...], w_ref[:, c0:c0 + cols])
        cos, sin = cos_ref[...], sin_ref[...]
        for c in range(n_qc + n_kc):
            sl = slice(c * LANES, (c + 1) * LANES)
            y = _rms(ep_ref[:, sl], gain_ref[:, sl])
            y = y * cos + pltpu.roll(y, LANES // 2, 1) * sin
            if c < n_qc:
                g, r = divmod(c, per_group)
                yt = y.T.astype(BF16)
                for j in range(tm // tq):
                    q_ref[g, j, :, r * tq:(r + 1) * tq] = yt[:, j * tq:(j + 1) * tq]
            else:
                k_ref[c - n_qc] = y.astype(BF16)
        for g in range((width - v0) // dv):
            v_ref[g] = ep_ref[:, v0 + g * dv:v0 + (g + 1) * dv].T.astype(BF16)

    @pl.when(i % 2 == 0)
    def _():
        step(acc_a, acc_b)

    @pl.when(i % 2 == 1)
    def _():
        step(acc_b, acc_a)


def _proj(h, w, gain, cos, sin, *, bsz, seq, n_qc, n_kc, per_group, dv, tq, tk, tm):
    m, d = h.shape
    width = w.shape[1]
    qk_w = (n_qc + n_kc) * LANES
    groups, v_groups = n_qc // per_group, (width - qk_w) // dv
    nt, per_chunk, n = seq // tm, tk // tm, m // tm
    assert tm % tq == 0 and tk % tm == 0 and seq % tk == 0 and (width - qk_w) % dv == 0

    def cur(i):
        return jnp.minimum(i, n - 1)

    def prev(i):
        return jnp.maximum(i - 1, 0)

    return pl.pallas_call(
        functools.partial(_proj_kernel, n_qc=n_qc, n_kc=n_kc, per_group=per_group, tq=tq),
        grid=(n + 1,),
        in_specs=[
            pl.BlockSpec((tm, d), lambda i: (cur(i), 0)),
            pl.BlockSpec((d, width), lambda i: (0, 0)),
            pl.BlockSpec((1, qk_w), lambda i: (0, 0)),
            pl.BlockSpec((tm, LANES), lambda i: (prev(i) % nt, 0)),
            pl.BlockSpec((tm, LANES), lambda i: (prev(i) % nt, 0)),
        ],
        out_specs=[
            pl.BlockSpec((None, groups, tm // tq, LANES, per_group * tq),
                         lambda i: (prev(i) // nt, 0, prev(i) % nt, 0, 0)),
            pl.BlockSpec((None, n_kc, tm, LANES), lambda i: (prev(i) // nt, 0, prev(i) % nt, 0)),
            pl.BlockSpec((None, v_groups, None, dv, tm),
                         lambda i: (prev(i) // nt, 0, (prev(i) % nt) // per_chunk, 0, (prev(i) % nt) % per_chunk)),
        ],
        out_shape=[
            jax.ShapeDtypeStruct((bsz, groups, seq // tq, LANES, per_group * tq), BF16),
            jax.ShapeDtypeStruct((bsz, n_kc, seq, LANES), BF16),
            jax.ShapeDtypeStruct((bsz, v_groups, seq // tk, dv, tk), BF16),
        ],
        scratch_shapes=[pltpu.VMEM((tm, width), F32), pltpu.VMEM((tm, width), F32)],
        compiler_params=_params("arbitrary"),
        name="proj",
    )(h, w, gain, cos, sin)


def _attn_kernel(*refs, tq, diff, lambda_init):
    if diff:
        q_ref, k_ref, v_ref, lq1_ref, lk1_ref, lq2_ref, lk2_ref, subln_ref, o_ref, acc_ref, m_ref, l_ref, l8_ref, kmax_ref = refs
    else:
        q_ref, k_ref, v_ref, o_ref, acc_ref, m_ref, l_ref, l8_ref, kmax_ref = refs
    n_k, n_chunks, tk = k_ref.shape[0], k_ref.shape[1], k_ref.shape[2]
    w = q_ref.shape[1]
    wk = w // n_k

    def scores(c):
        parts = [_dot(k_ref[ks, c], q_ref[:, ks * wk:(ks + 1) * wk]) for ks in range(n_k)]
        return parts[0] if n_k == 1 else jnp.concatenate(parts, axis=1)

    @pl.when(pl.program_id(2) == 0)
    def _():
        for ks in range(n_k):
            def kbody(c, best):
                kf = k_ref[ks, c].astype(F32)
                return jnp.maximum(best, jnp.max(jnp.sum(kf * kf, axis=-1, keepdims=True), axis=0, keepdims=True))
            k2 = lax.fori_loop(0, n_chunks, kbody, jnp.zeros((1, 1), F32))
            kmax_ref[ks] = jnp.broadcast_to(jnp.sqrt(k2), (1, LANES))

    qf = q_ref[...].astype(F32)
    qn = jnp.sqrt(jnp.sum(qf * qf, axis=0, keepdims=True))
    for ks in range(n_k):
        m_ref[:, ks * wk:(ks + 1) * wk] = qn[:, ks * wk:(ks + 1) * wk] * kmax_ref[ks, :, 0:1]
    acc_ref[...] = jnp.zeros(acc_ref.shape, F32)
    l8_ref[...] = jnp.zeros(l8_ref.shape, F32)

    def fast_body(c, carry):
        p = jnp.exp2(scores(c) - m_ref[...])
        l8_ref[...] += jnp.sum(p.reshape(tk // 8, 8, w), axis=0)
        acc_ref[...] += _dot(v_ref[c], p.astype(BF16))
        return carry

    lax.fori_loop(0, n_chunks, fast_body, 0)
    l_ref[...] = jnp.sum(l8_ref[...], axis=0, keepdims=True)
    underflow = jnp.min(l_ref[...]) < L_MIN_SAFE

    @pl.when(underflow)
    def _():
        acc_ref[...] = jnp.zeros(acc_ref.shape, F32)
        m_ref[...] = jnp.full(m_ref.shape, NEG_BIG, F32)
        l_ref[...] = jnp.zeros(l_ref.shape, F32)

        def body(c, carry):
            s = scores(c)
            m_old = m_ref[...]
            m_new = jnp.maximum(m_old, jnp.max(s, axis=0, keepdims=True))
            alpha = jnp.exp2(m_old - m_new)
            p = jnp.exp2(s - m_new)
            l_ref[...] = alpha * l_ref[...] + jnp.sum(p, axis=0, keepdims=True)
            acc_ref[...] = alpha * acc_ref[...] + _dot(v_ref[c], p.astype(BF16))
            m_ref[...] = m_new
            return carry

        lax.fori_loop(0, n_chunks, body, 0)

    o = acc_ref[...] * (1.0 / l_ref[...])
    if not diff:
        for r in range(w // tq):
            o_ref[:, r * LANES:(r + 1) * LANES] = o[:, r * tq:(r + 1) * tq].T.astype(BF16)
    else:
        lam = (jnp.exp(jnp.sum(lq1_ref[...] * lk1_ref[...], axis=-1, keepdims=True))
               - jnp.exp(jnp.sum(lq2_ref[...] * lk2_ref[...], axis=-1, keepdims=True))
               + lambda_init)
        d = o[:, :tq] - lam * o[:, tq:]
        y = _rms(d.T, subln_ref[...]) * (1.0 - lambda_init)
        o_ref[...] = y.astype(BF16)


def _attn(qt, k, vt, *, tq, diff_args=None, lambda_init=0.0):
    b, g, nqt, hd, w = qt.shape
    n_k, nc, tk = k.shape[2], k.shape[3], k.shape[4]
    dv = vt.shape[3]
    diff = diff_args is not None
    width = dv if diff else (w // tq) * dv
    in_specs = [
        pl.BlockSpec((None, None, None, hd, w), lambda bi, gi, qi: (bi, gi, qi, 0, 0)),
        pl.BlockSpec((None, None, n_k, nc, tk, hd), lambda bi, gi, qi: (bi, gi, 0, 0, 0, 0)),
        pl.BlockSpec((None, None, nc, dv, tk), lambda bi, gi, qi: (bi, gi, 0, 0, 0)),
    ]
    args = [qt, k, vt]
    if diff:
        in_specs += [pl.BlockSpec((1, hd), lambda bi, gi, qi: (0, 0))] * 4
        in_specs += [pl.BlockSpec((1, dv), lambda bi, gi, qi: (0, 0))]
        args += list(diff_args)
    return pl.pallas_call(
        functools.partial(_attn_kernel, tq=tq, diff=diff, lambda_init=lambda_init),
        grid=(b, g, nqt),
        in_specs=in_specs,
        out_specs=pl.BlockSpec((None, tq, width), lambda bi, gi, qi: (bi, qi, gi)),
        out_shape=jax.ShapeDtypeStruct((b, nqt * tq, g * width), BF16),
        scratch_shapes=[
            pltpu.VMEM((dv, w), F32),
            pltpu.VMEM((1, w), F32),
            pltpu.VMEM((1, w), F32),
            pltpu.VMEM((8, w), F32),
            pltpu.VMEM((n_k, 1, LANES), F32),
        ],
        compiler_params=_params("parallel", "parallel", "arbitrary"),
        name="attn_diff" if diff else "attn_gqa",
    )(*args)


def _outproj_kernel(x_ref, a_ref, b_ref, wa_ref, wb_ref, o_ref):
    o_ref[...] = x_ref[...] + _dot(a_ref[...], wa_ref[...]) + _dot(b_ref[...], wb_ref[...])


def _outproj(x, a, b, w_out, *, tm):
    m, d = x.shape
    da, db = a.shape[1], b.shape[1]
    assert da == db and w_out.shape[0] == da + db
    return pl.pallas_call(
        _outproj_kernel,
        grid=(m // tm,),
        in_specs=[
            pl.BlockSpec((tm, d), lambda i: (i, 0)),
            pl.BlockSpec((tm, da), lambda i: (i, 0)),
            pl.BlockSpec((tm, db), lambda i: (i, 0)),
            pl.BlockSpec((da, d), lambda i: (0, 0)),
            pl.BlockSpec((db, d), lambda i: (1, 0)),
        ],
        out_specs=pl.BlockSpec((tm, d), lambda i: (i, 0)),
        out_shape=jax.ShapeDtypeStruct((m, d), F32),
        compiler_params=_params("parallel"),
        name="outproj",
    )(x, a, b, w_out, w_out)


def _rope_cs(pos, dim, theta):
    inv_freq = theta ** (-jnp.arange(0, dim, 2, dtype=F32) / dim)
    ang = pos[:, None] * inv_freq[None, :]
    return jnp.cos(ang), jnp.sin(ang)


ROPE_BLOCK = 16
AXIAL_ORDER = (0, 1, 4, 5, 2, 3, 6, 7)
PARTIAL_ORDER = (0, 2, 3, 4, 1, 5, 6, 7)


def _permute_heads(a, order):
    lead = a.shape[:-1]
    a4 = a.reshape(lead + (a.shape[-1] // HEAD_DIM, HEAD_DIM // ROPE_BLOCK, ROPE_BLOCK))
    return jnp.concatenate([a4[..., i:i + 1, :] for i in order], axis=-2).reshape(a.shape)


def _rope_tables(seq):
    rows = seq // GRID_W
    rc, rs = (jnp.repeat(a, GRID_W, axis=0) for a in _rope_cs(jnp.arange(rows, dtype=F32), AXIAL_DIM, A_ROPE_THETA))
    cc, cs = (jnp.tile(a, (rows, 1)) for a in _rope_cs(jnp.arange(GRID_W, dtype=F32), AXIAL_DIM, A_ROPE_THETA))
    pc, ps = _rope_cs(jnp.arange(seq, dtype=F32), PARTIAL_ROPE_DIM, PARTIAL_ROPE_THETA)
    cos_a = jnp.concatenate([rc, cc, rc, cc], axis=-1)
    sin_a = jnp.concatenate([-rs, -cs, rs, cs], axis=-1)
    rest = HEAD_DIM // 2 - PARTIAL_ROPE_DIM // 2
    one, zero = jnp.ones((seq, rest), F32), jnp.zeros((seq, rest), F32)
    cos_b = jnp.concatenate([pc, one, pc, one], axis=-1)
    sin_b = jnp.concatenate([-ps, zero, ps, zero], axis=-1)
    return (cos_a, sin_a), (cos_b, sin_b)


def _tiles(m, seq, d_ff):
    def fit(n, t):
        t = min(t, n)
        while n % t:
            t //= 2
        return t
    return dict(
        ffn1_tm=fit(m, 1024), ffn2_tm=fit(m, 1024), ffn_tf=fit(d_ff, 512),
        proj_tm=fit(seq, 512), out_tm=fit(m, 512),
        a_tq=fit(seq, 256), b_tq=fit(seq, 512), a_tk=fit(seq, 8192), b_tk=fit(seq, 4096),
    )


def kernel(x, ffn1_norm, ffn1_w_gu, ffn1_w_down, mix_norm, w_in, a_q_norm, a_k_norm, b_q_norm, b_k_norm,
           b_lambda_q1, b_lambda_k1, b_lambda_q2, b_lambda_k2, b_subln, w_out, ffn2_norm, ffn2_w_gu,
           ffn2_w_down, out_norm):
    bsz, seq, d = x.shape
    depth = w_in.shape[0]
    d_ff = ffn1_w_down.shape[1]
    m = bsz * seq
    hd = HEAD_DIM
    a_heads = d // (2 * hd)
    a_kv = a_heads // A_GROUP
    b_vdim = 2 * hd
    b_heads = d // (2 * b_vdim)
    a_q, a_kvw = a_heads * hd, a_kv * hd
    b_qk, b_v = b_heads * 2 * hd, b_heads * b_vdim
    t = _tiles(m, seq, d_ff)
    q_scale = (hd ** -0.5) * LOG2E
    tab_a, tab_b = _rope_tables(seq)

    x = x.reshape(m, d)
    for l in range(depth):
        lambda_init = 0.8 - 0.6 * math.exp(-0.3 * l)
        row = lambda v: v[l].reshape(1, -1).astype(F32)
        w_gu1, w_d1 = ffn1_w_gu[l].astype(BF16), ffn1_w_down[l].astype(BF16)
        w_gu2, w_d2 = ffn2_w_gu[l].astype(BF16), ffn2_w_down[l].astype(BF16)
        w_o = w_out[l].astype(BF16)
        c1 = a_q + a_kvw
        c2 = c1 + a_kvw
        c3 = c2 + 2 * b_qk
        w_i = w_in[l]
        w_a = jnp.concatenate([_permute_heads(w_i[:, :c1], AXIAL_ORDER), w_i[:, c1:c2]], axis=1).astype(BF16)
        w_b = jnp.concatenate([_permute_heads(w_i[:, c2:c3], PARTIAL_ORDER), w_i[:, c3:]], axis=1).astype(BF16)

        x, h = _ffn(x, row(ffn1_norm), w_gu1, w_d1, row(mix_norm),
                    emit_norm=True, final_norm=False, tm=t["ffn1_tm"], tf=t["ffn_tf"])

        tqa, tqb, tka, tkb, ptm = t["a_tq"], t["b_tq"], t["a_tk"], t["b_tk"], t["proj_tm"]
        gain_a = jnp.concatenate([jnp.tile(_permute_heads(row(a_q_norm), AXIAL_ORDER) * q_scale, (1, a_heads)),
                                  jnp.tile(_permute_heads(row(a_k_norm), AXIAL_ORDER), (1, a_kv))], axis=-1)
        qa_t, ka, va_t = _proj(h, w_a, gain_a, *tab_a, bsz=bsz, seq=seq, n_qc=a_heads, n_kc=a_kv,
                               per_group=A_GROUP, dv=hd, tq=tqa, tk=tka, tm=ptm)
        gain_b = jnp.concatenate([jnp.tile(_permute_heads(row(b_q_norm), PARTIAL_ORDER) * q_scale, (1, 2 * b_heads)),
                                  jnp.tile(_permute_heads(row(b_k_norm), PARTIAL_ORDER), (1, 2 * b_heads))], axis=-1)
        qb_t, kb, vb_t = _proj(h, w_b, gain_b, *tab_b, bsz=bsz, seq=seq, n_qc=2 * b_heads, n_kc=2 * b_heads,
                               per_group=2, dv=b_vdim, tq=tqb, tk=tkb, tm=ptm)
        ka = ka.reshape(bsz, a_kv, 1, seq // tka, tka, hd)
        kb = kb.reshape(bsz, b_heads, 2, seq // tkb, tkb, hd)

        out_a = _attn(qa_t, ka, va_t, tq=tqa)
        out_b = _attn(qb_t, kb, vb_t, tq=tqb,
                      diff_args=(row(b_lambda_q1), row(b_lambda_k1), row(b_lambda_q2), row(b_lambda_k2), row(b_subln)),
                      lambda_init=lambda_init)

        x = _outproj(x, out_a.reshape(m, a_q), out_b.reshape(m, b_v), w_o, tm=t["out_tm"])
        x, = _ffn(x, row(ffn2_norm), w_gu2, w_d2, row(out_norm),
                  emit_norm=False, final_norm=True, tm=t["ffn2_tm"], tf=t["ffn_tf"])
    return x.reshape(bsz, seq, d)
```

```python
import functools
import math

import jax
import jax.numpy as jnp
from jax import lax
from jax.experimental import pallas as pl
from jax.experimental.pallas import tpu as pltpu

HEAD_DIM = 128
GRID_W = 64
EPS = 1e-6
A_ROPE_THETA = 10000.0
PARTIAL_ROPE_THETA = 500000.0
AXIAL_DIM = HEAD_DIM // 2
PARTIAL_ROPE_DIM = HEAD_DIM // 4
A_GROUP = 4
LANES = 128
LOG2E = math.log2(math.e)
VMEM_LIMIT_BYTES = 63 * 1024 * 1024
NEG_BIG = -1e30
L_MIN_SAFE = 1e-20

BF16 = jnp.bfloat16
F32 = jnp.float32


def _dot(a, b):
    return jnp.dot(a, b, preferred_element_type=F32)


def _rms(x, gain):
    r = lax.rsqrt(jnp.mean(x * x, axis=-1, keepdims=True) + EPS)
    return (x * r) * gain


def _params(*semantics):
    return pltpu.CompilerParams(dimension_semantics=semantics, vmem_limit_bytes=VMEM_LIMIT_BYTES)


def _ffn_kernel(x_ref, gain_ref, wg_ref, wu_ref, wd_ref, ngain_ref, o_ref, *rest, emit_norm, final_norm):
    if emit_norm:
        hn_ref, xn_ref = rest
    else:
        (xn_ref,) = rest
    j = pl.program_id(1)

    @pl.when(j == 0)
    def _():
        x = x_ref[...]
        xn_ref[...] = _rms(x, gain_ref[...]).astype(BF16)
        o_ref[...] = x

    xn = xn_ref[...]
    g = _dot(xn, wg_ref[...])
    u = _dot(xn, wu_ref[...])
    act = (0.5 * g / (1.0 + jnp.exp(-g))) * u
    o_ref[...] += _dot(act.astype(BF16), wd_ref[...])

    @pl.when(j == pl.num_programs(1) - 1)
    def _():
        y = _rms(o_ref[...], ngain_ref[...])
        if emit_norm:
            hn_ref[...] = y.astype(BF16)
        if final_norm:
            o_ref[...] = y


def _ffn(x, gain, w_gu, w_down, next_gain, *, emit_norm, final_norm, tm, tf):
    m, d = x.shape
    d_ff = w_down.shape[0]
    nf = d_ff // tf
    assert m % tm == 0 and d_ff % tf == 0
    out_shape = [jax.ShapeDtypeStruct((m, d), F32)]
    out_specs = [pl.BlockSpec((tm, d), lambda i, j: (i, 0))]
    if emit_norm:
        out_shape.append(jax.ShapeDtypeStruct((m, d), BF16))
        out_specs.append(pl.BlockSpec((tm, d), lambda i, j: (i, 0)))
    res = pl.pallas_call(
        functools.partial(_ffn_kernel, emit_norm=emit_norm, final_norm=final_norm),
        grid=(m // tm, nf),
        in_specs=[
            pl.BlockSpec((tm, d), lambda i, j: (i, 0)),
            pl.BlockSpec((1, d), lambda i, j: (0, 0)),
            pl.BlockSpec((d, tf), lambda i, j: (0, j)),
            pl.BlockSpec((d, tf), lambda i, j: (0, j + nf)),
            pl.BlockSpec((tf, d), lambda i, j: (j, 0)),
            pl.BlockSpec((1, d), lambda i, j: (0, 0)),
        ],
        out_specs=out_specs,
        out_shape=out_shape,
        scratch_shapes=[pltpu.VMEM((tm, d), BF16)],
        compiler_params=_params("parallel", "arbitrary"),
        name="ffn",
    )(x, gain, w_gu, w_gu, w_down, next_gain)
    return res


PROJ_COLS = 2 * LANES


def _proj_kernel(h_ref, w_ref, gain_ref, cos_ref, sin_ref, q_ref, k_ref, v_ref, acc_a, acc_b, *,
                 n_qc, n_kc, per_group, tq):
    i = pl.program_id(0)
    tm = h_ref.shape[0]
    width = w_ref.shape[1]
    dv = v_ref.shape[1]
    v0 = (n_qc + n_kc) * LANES

    @pl.when(i == 0)
    def _():
        acc_b[...] = jnp.zeros(acc_b.shape, F32)

    def step(mm_ref, ep_ref):
        for c0 in range(0, width, PROJ_COLS):
            cols = min(PROJ_COLS, width - c0)
            mm_ref[:, c0:c0 + cols] = _dot(h_ref[...], w_ref[:, c0:c0 + cols])
        cos, sin = cos_ref[...], sin_ref[...]
        for c in range(n_qc + n_kc):
            sl = slice(c * LANES, (c + 1) * LANES)
            y = _rms(ep_ref[:, sl], gain_ref[:, sl])
            y = y * cos + pltpu.roll(y, LANES // 2, 1) * sin
            if c < n_qc:
                g, r = divmod(c, per_group)
                yt = y.T.astype(BF16)
                for j in range(tm // tq):
                    q_ref[g, j, :, r * tq:(r + 1) * tq] = yt[:, j * tq:(j + 1) * tq]
            else:
                k_ref[c - n_qc] = y.astype(BF16)
        for g in range((width - v0) // dv):
            v_ref[g] = ep_ref[:, v0 + g * dv:v0 + (g + 1) * dv].T.astype(BF16)

    @pl.when(i % 2 == 0)
    def _():
        step(acc_a, acc_b)

    @pl.when(i % 2 == 1)
    def _():
        step(acc_b, acc_a)


def _proj(h, w, gain, cos, sin, *, bsz, seq, n_qc, n_kc, per_group, dv, tq, tk, tm):
    m, d = h.shape
    width = w.shape[1]
    qk_w = (n_qc + n_kc) * LANES
    groups, v_groups = n_qc // per_group, (width - qk_w) // dv
    nt, per_chunk, n = seq // tm, tk // tm, m // tm
    assert tm % tq == 0 and tk % tm == 0 and seq % tk == 0 and (width - qk_w) % dv == 0

    def cur(i):
        return jnp.minimum(i, n - 1)

    def prev(i):
        return jnp.maximum(i - 1, 0)

    return pl.pallas_call(
        functools.partial(_proj_kernel, n_qc=n_qc, n_kc=n_kc, per_group=per_group, tq=tq),
        grid=(n + 1,),
        in_specs=[
            pl.BlockSpec((tm, d), lambda i: (cur(i), 0)),
            pl.BlockSpec((d, width), lambda i: (0, 0)),
            pl.BlockSpec((1, qk_w), lambda i: (0, 0)),
            pl.BlockSpec((tm, LANES), lambda i: (prev(i) % nt, 0)),
            pl.BlockSpec((tm, LANES), lambda i: (prev(i) % nt, 0)),
        ],
        out_specs=[
            pl.BlockSpec((None, groups, tm // tq, LANES, per_group * tq),
                         lambda i: (prev(i) // nt, 0, prev(i) % nt, 0, 0)),
            pl.BlockSpec((None, n_kc, tm, LANES), lambda i: (prev(i) // nt, 0, prev(i) % nt, 0)),
            pl.BlockSpec((None, v_groups, None, dv, tm),
                         lambda i: (prev(i) // nt, 0, (prev(i) % nt) // per_chunk, 0, (prev(i) % nt) % per_chunk)),
        ],
        out_shape=[
            jax.ShapeDtypeStruct((bsz, groups, seq // tq, LANES, per_group * tq), BF16),
            jax.ShapeDtypeStruct((bsz, n_kc, seq, LANES), BF16),
            jax.ShapeDtypeStruct((bsz, v_groups, seq // tk, dv, tk), BF16),
        ],
        scratch_shapes=[pltpu.VMEM((tm, width), F32), pltpu.VMEM((tm, width), F32)],
        compiler_params=_params("arbitrary"),
        name="proj",
    )(h, w, gain, cos, sin)


def _attn_kernel(*refs, tq, diff, lambda_init):
    if diff:
        q_ref, k_ref, v_ref, lq1_ref, lk1_ref, lq2_ref, lk2_ref, subln_ref, o_ref, acc_ref, m_ref, l_ref, l8_ref, kmax_ref = refs
    else:
        q_ref, k_ref, v_ref, o_ref, acc_ref, m_ref, l_ref, l8_ref, kmax_ref = refs
    n_k, n_chunks, tk = k_ref.shape[0], k_ref.shape[1], k_ref.shape[2]
    w = q_ref.shape[1]
    wk = w // n_k

    def scores(c):
        parts = [_dot(k_ref[ks, c], q_ref[:, ks * wk:(ks + 1) * wk]) for ks in range(n_k)]
        return parts[0] if n_k == 1 else jnp.concatenate(parts, axis=1)

    @pl.when(pl.program_id(2) == 0)
    def _():
        for ks in range(n_k):
            def kbody(c, best):
                kf = k_ref[ks, c].astype(F32)
                return jnp.maximum(best, jnp.max(jnp.sum(kf * kf, axis=-1, keepdims=True), axis=0, keepdims=True))
            k2 = lax.fori_loop(0, n_chunks, kbody, jnp.zeros((1, 1), F32))
            kmax_ref[ks] = jnp.broadcast_to(jnp.sqrt(k2), (1, LANES))

    qf = q_ref[...].astype(F32)
    qn = jnp.sqrt(jnp.sum(qf * qf, axis=0, keepdims=True))
    for ks in range(n_k):
        m_ref[:, ks * wk:(ks + 1) * wk] = qn[:, ks * wk:(ks + 1) * wk] * kmax_ref[ks, :, 0:1]
    acc_ref[...] = jnp.zeros(acc_ref.shape, F32)
    l8_ref[...] = jnp.zeros(l8_ref.shape, F32)

    def fast_body(c, carry):
        p = jnp.exp2(scores(c) - m_ref[...])
        l8_ref[...] += jnp.sum(p.reshape(tk // 8, 8, w), axis=0)
        acc_ref[...] += _dot(v_ref[c], p.astype(BF16))
        return carry

    lax.fori_loop(0, n_chunks, fast_body, 0)
    l_ref[...] = jnp.sum(l8_ref[...], axis=0, keepdims=True)
    underflow = jnp.min(l_ref[...]) < L_MIN_SAFE

    @pl.when(underflow)
    def _():
        acc_ref[...] = jnp.zeros(acc_ref.shape, F32)
        m_ref[...] = jnp.full(m_ref.shape, NEG_BIG, F32)
        l_ref[...] = jnp.zeros(l_ref.shape, F32)

        def body(c, carry):
            s = scores(c)
            m_old = m_ref[...]
            m_new = jnp.maximum(m_old, jnp.max(s, axis=0, keepdims=True))
            alpha = jnp.exp2(m_old - m_new)
            p = jnp.exp2(s - m_new)
            l_ref[...] = alpha * l_ref[...] + jnp.sum(p, axis=0, keepdims=True)
            acc_ref[...] = alpha * acc_ref[...] + _dot(v_ref[c], p.astype(BF16))
            m_ref[...] = m_new
            return carry

        lax.fori_loop(0, n_chunks, body, 0)

    o = acc_ref[...] * (1.0 / l_ref[...])
    if not diff:
        for r in range(w // tq):
            o_ref[:, r * LANES:(r + 1) * LANES] = o[:, r * tq:(r + 1) * tq].T.astype(BF16)
    else:
        lam = (jnp.exp(jnp.sum(lq1_ref[...] * lk1_ref[...], axis=-1, keepdims=True))
               - jnp.exp(jnp.sum(lq2_ref[...] * lk2_ref[...], axis=-1, keepdims=True))
               + lambda_init)
        d = o[:, :tq] - lam * o[:, tq:]
        y = _rms(d.T, subln_ref[...]) * (1.0 - lambda_init)
        o_ref[...] = y.astype(BF16)


def _attn(qt, k, vt, *, tq, diff_args=None, lambda_init=0.0):
    b, g, nqt, hd, w = qt.shape
    n_k, nc, tk = k.shape[2], k.shape[3], k.shape[4]
    dv = vt.shape[3]
    diff = diff_args is not None
    width = dv if diff else (w // tq) * dv
    in_specs = [
        pl.BlockSpec((None, None, None, hd, w), lambda bi, gi, qi: (bi, gi, qi, 0, 0)),
        pl.BlockSpec((None, None, n_k, nc, tk, hd), lambda bi, gi, qi: (bi, gi, 0, 0, 0, 0),
                     pipeline_mode=pl.Buffered(1)),
        pl.BlockSpec((None, None, nc, dv, tk), lambda bi, gi, qi: (bi, gi, 0, 0, 0),
                     pipeline_mode=pl.Buffered(1)),
    ]
    args = [qt, k, vt]
    if diff:
        in_specs += [pl.BlockSpec((1, hd), lambda bi, gi, qi: (0, 0))] * 4
        in_specs += [pl.BlockSpec((1, dv), lambda bi, gi, qi: (0, 0))]
        args += list(diff_args)
    return pl.pallas_call(
        functools.partial(_attn_kernel, tq=tq, diff=diff, lambda_init=lambda_init),
        grid=(b, g, nqt),
        in_specs=in_specs,
        out_specs=pl.BlockSpec((None, tq, width), lambda bi, gi, qi: (bi, qi, gi)),
        out_shape=jax.ShapeDtypeStruct((b, nqt * tq, g * width), BF16),
        scratch_shapes=[
            pltpu.VMEM((dv, w), F32),
            pltpu.VMEM((1, w), F32),
            pltpu.VMEM((1, w), F32),
            pltpu.VMEM((8, w), F32),
            pltpu.VMEM((n_k, 1, LANES), F32),
        ],
        compiler_params=_params("parallel", "parallel", "arbitrary"),
        name="attn_diff" if diff else "attn_gqa",
    )(*args)


def _outproj_kernel(x_ref, a_ref, b_ref, wa_ref, wb_ref, o_ref):
    o_ref[...] = x_ref[...] + _dot(a_ref[...], wa_ref[...]) + _dot(b_ref[...], wb_ref[...])


def _outproj(x, a, b, w_out, *, tm):
    m, d = x.shape
    da, db = a.shape[1], b.shape[1]
    assert da == db and w_out.shape[0] == da + db
    return pl.pallas_call(
        _outproj_kernel,
        grid=(m // tm,),
        in_specs=[
            pl.BlockSpec((tm, d), lambda i: (i, 0)),
            pl.BlockSpec((tm, da), lambda i: (i, 0)),
            pl.BlockSpec((tm, db), lambda i: (i, 0)),
            pl.BlockSpec((da, d), lambda i: (0, 0)),
            pl.BlockSpec((db, d), lambda i: (1, 0)),
        ],
        out_specs=pl.BlockSpec((tm, d), lambda i: (i, 0)),
        out_shape=jax.ShapeDtypeStruct((m, d), F32),
        compiler_params=_params("parallel"),
        name="outproj",
    )(x, a, b, w_out, w_out)


def _rope_cs(pos, dim, theta):
    inv_freq = theta ** (-jnp.arange(0, dim, 2, dtype=F32) / dim)
    ang = pos[:, None] * inv_freq[None, :]
    return jnp.cos(ang), jnp.sin(ang)


ROPE_BLOCK = 16
AXIAL_ORDER = (0, 1, 4, 5, 2, 3, 6, 7)
PARTIAL_ORDER = (0, 2, 3, 4, 1, 5, 6, 7)


def _permute_heads(a, order):
    lead = a.shape[:-1]
    a4 = a.reshape(lead + (a.shape[-1] // HEAD_DIM, HEAD_DIM // ROPE_BLOCK, ROPE_BLOCK))
    return jnp.concatenate([a4[..., i:i + 1, :] for i in order], axis=-2).reshape(a.shape)


def _rope_tables(seq):
    rows = seq // GRID_W
    rc, rs = (jnp.repeat(a, GRID_W, axis=0) for a in _rope_cs(jnp.arange(rows, dtype=F32), AXIAL_DIM, A_ROPE_THETA))
    cc, cs = (jnp.tile(a, (rows, 1)) for a in _rope_cs(jnp.arange(GRID_W, dtype=F32), AXIAL_DIM, A_ROPE_THETA))
    pc, ps = _rope_cs(jnp.arange(seq, dtype=F32), PARTIAL_ROPE_DIM, PARTIAL_ROPE_THETA)
    cos_a = jnp.concatenate([rc, cc, rc, cc], axis=-1)
    sin_a = jnp.concatenate([-rs, -cs, rs, cs], axis=-1)
    rest = HEAD_DIM // 2 - PARTIAL_ROPE_DIM // 2
    one, zero = jnp.ones((seq, rest), F32), jnp.zeros((seq, rest), F32)
    cos_b = jnp.concatenate([pc, one, pc, one], axis=-1)
    sin_b = jnp.concatenate([-ps, zero, ps, zero], axis=-1)
    return (cos_a, sin_a), (cos_b, sin_b)


def _tiles(m, seq, d_ff):
    def fit(n, t):
        t = min(t, n)
        while n % t:
            t //= 2
        return t
    return dict(
        ffn1_tm=fit(m, 1024), ffn2_tm=fit(m, 1024), ffn_tf=fit(d_ff, 512),
        proj_tm=fit(seq, 512), out_tm=fit(m, 512),
        a_tq=fit(seq, 256), b_tq=fit(seq, 512), a_tk=fit(seq, 8192), b_tk=fit(seq, 8192),
    )


def kernel(x, ffn1_norm, ffn1_w_gu, ffn1_w_down, mix_norm, w_in, a_q_norm, a_k_norm, b_q_norm, b_k_norm,
           b_lambda_q1, b_lambda_k1, b_lambda_q2, b_lambda_k2, b_subln, w_out, ffn2_norm, ffn2_w_gu,
           ffn2_w_down, out_norm):
    bsz, seq, d = x.shape
    depth = w_in.shape[0]
    d_ff = ffn1_w_down.shape[1]
    m = bsz * seq
    hd = HEAD_DIM
    a_heads = d // (2 * hd)
    a_kv = a_heads // A_GROUP
    b_vdim = 2 * hd
    b_heads = d // (2 * b_vdim)
    a_q, a_kvw = a_heads * hd, a_kv * hd
    b_qk, b_v = b_heads * 2 * hd, b_heads * b_vdim
    t = _tiles(m, seq, d_ff)
    q_scale = (hd ** -0.5) * LOG2E
    tab_a, tab_b = _rope_tables(seq)

    x = x.reshape(m, d)
    for l in range(depth):
        lambda_init = 0.8 - 0.6 * math.exp(-0.3 * l)
        row = lambda v: v[l].reshape(1, -1).astype(F32)
        w_gu1, w_d1 = ffn1_w_gu[l].astype(BF16), ffn1_w_down[l].astype(BF16)
        w_gu2, w_d2 = ffn2_w_gu[l].astype(BF16), ffn2_w_down[l].astype(BF16)
        w_o = w_out[l].astype(BF16)
        c1 = a_q + a_kvw
        c2 = c1 + a_kvw
        c3 = c2 + 2 * b_qk
        w_i = w_in[l]
        w_a = jnp.concatenate([_permute_heads(w_i[:, :c1], AXIAL_ORDER), w_i[:, c1:c2]], axis=1).astype(BF16)
        w_b = jnp.concatenate([_permute_heads(w_i[:, c2:c3], PARTIAL_ORDER), w_i[:, c3:]], axis=1).astype(BF16)

        x, h = _ffn(x, row(ffn1_norm), w_gu1, w_d1, row(mix_norm),
                    emit_norm=True, final_norm=False, tm=t["ffn1_tm"], tf=t["ffn_tf"])

        tqa, tqb, tka, tkb, ptm = t["a_tq"], t["b_tq"], t["a_tk"], t["b_tk"], t["proj_tm"]
        gain_a = jnp.concatenate([jnp.tile(_permute_heads(row(a_q_norm), AXIAL_ORDER) * q_scale, (1, a_heads)),
                                  jnp.tile(_permute_heads(row(a_k_norm), AXIAL_ORDER), (1, a_kv))], axis=-1)
        qa_t, ka, va_t = _proj(h, w_a, gain_a, *tab_a, bsz=bsz, seq=seq, n_qc=a_heads, n_kc=a_kv,
                               per_group=A_GROUP, dv=hd, tq=tqa, tk=tka, tm=ptm)
        gain_b = jnp.concatenate([jnp.tile(_permute_heads(row(b_q_norm), PARTIAL_ORDER) * q_scale, (1, 2 * b_heads)),
                                  jnp.tile(_permute_heads(row(b_k_norm), PARTIAL_ORDER), (1, 2 * b_heads))], axis=-1)
        qb_t, kb, vb_t = _proj(h, w_b, gain_b, *tab_b, bsz=bsz, seq=seq, n_qc=2 * b_heads, n_kc=2 * b_heads,
                               per_group=2, dv=b_vdim, tq=tqb, tk=tkb, tm=ptm)
        ka = ka.reshape(bsz, a_kv, 1, seq // tka, tka, hd)
        kb = kb.reshape(bsz, b_heads, 2, seq // tkb, tkb, hd)

        out_a = _attn(qa_t, ka, va_t, tq=tqa)
        out_b = _attn(qb_t, kb, vb_t, tq=tqb,
                      diff_args=(row(b_lambda_q1), row(b_lambda_k1), row(b_lambda_q2), row(b_lambda_k2), row(b_subln)),
                      lambda_init=lambda_init)

        x = _outproj(x, out_a.reshape(m, a_q), out_b.reshape(m, b_v), w_o, tm=t["out_tm"])
        x, = _ffn(x, row(ffn2_norm), w_gu2, w_d2, row(out_norm),
                  emit_norm=False, final_norm=True, tm=t["ffn2_tm"], tf=t["ffn_tf"])
    return x.reshape(bsz, seq, d)
```

```python
import functools
import math

import jax
import jax.numpy as jnp
from jax import lax
from jax.experimental import pallas as pl
from jax.experimental.pallas import tpu as pltpu

HEAD_DIM = 128
GRID_W = 64
EPS = 1e-6
A_ROPE_THETA = 10000.0
PARTIAL_ROPE_THETA = 500000.0
AXIAL_DIM = HEAD_DIM // 2
PARTIAL_ROPE_DIM = HEAD_DIM // 4
A_GROUP = 4
LANES = 128
LOG2E = math.log2(math.e)
VMEM_LIMIT_BYTES = 63 * 1024 * 1024
NEG_BIG = -1e30
L_MIN_SAFE = 1e-20

BF16 = jnp.bfloat16
F32 = jnp.float32


def _dot(a, b):
    return jnp.dot(a, b, preferred_element_type=F32)


def _rms(x, gain):
    r = lax.rsqrt(jnp.mean(x * x, axis=-1, keepdims=True) + EPS)
    return (x * r) * gain


def _params(*semantics):
    return pltpu.CompilerParams(dimension_semantics=semantics, vmem_limit_bytes=VMEM_LIMIT_BYTES)


def _ffn_kernel(x_ref, gain_ref, wg_ref, wu_ref, wd_ref, ngain_ref, o_ref, *rest, emit_norm, final_norm):
    if emit_norm:
        hn_ref, xn_ref = rest
    else:
        (xn_ref,) = rest
    j = pl.program_id(1)

    @pl.when(j == 0)
    def _():
        x = x_ref[...]
        xn_ref[...] = _rms(x, gain_ref[...]).astype(BF16)
        o_ref[...] = x

    xn = xn_ref[...]
    g = _dot(xn, wg_ref[...])
    u = _dot(xn, wu_ref[...])
    act = (0.5 * g / (1.0 + jnp.exp(-g))) * u
    o_ref[...] += _dot(act.astype(BF16), wd_ref[...])

    @pl.when(j == pl.num_programs(1) - 1)
    def _():
        y = _rms(o_ref[...], ngain_ref[...])
        if emit_norm:
            hn_ref[...] = y.astype(BF16)
        if final_norm:
            o_ref[...] = y


def _ffn(x, gain, w_gu, w_down, next_gain, *, emit_norm, final_norm, tm, tf):
    m, d = x.shape
    d_ff = w_down.shape[0]
    nf = d_ff // tf
    assert m % tm == 0 and d_ff % tf == 0
    out_shape = [jax.ShapeDtypeStruct((m, d), F32)]
    out_specs = [pl.BlockSpec((tm, d), lambda i, j: (i, 0))]
    if emit_norm:
        out_shape.append(jax.ShapeDtypeStruct((m, d), BF16))
        out_specs.append(pl.BlockSpec((tm, d), lambda i, j: (i, 0)))
    res = pl.pallas_call(
        functools.partial(_ffn_kernel, emit_norm=emit_norm, final_norm=final_norm),
        grid=(m // tm, nf),
        in_specs=[
            pl.BlockSpec((tm, d), lambda i, j: (i, 0)),
            pl.BlockSpec((1, d), lambda i, j: (0, 0)),
            pl.BlockSpec((d, tf), lambda i, j: (0, j)),
            pl.BlockSpec((d, tf), lambda i, j: (0, j + nf)),
            pl.BlockSpec((tf, d), lambda i, j: (j, 0)),
            pl.BlockSpec((1, d), lambda i, j: (0, 0)),
        ],
        out_specs=out_specs,
        out_shape=out_shape,
        scratch_shapes=[pltpu.VMEM((tm, d), BF16)],
        compiler_params=_params("parallel", "arbitrary"),
        name="ffn",
    )(x, gain, w_gu, w_gu, w_down, next_gain)
    return res


PROJ_COLS = 2 * LANES


def _proj_kernel(h_ref, w_ref, gain_ref, cos_ref, sin_ref, q_ref, k_ref, v_ref, acc_a, acc_b, *,
                 n_qc, n_kc, per_group, tq):
    i = pl.program_id(0)
    tm = h_ref.shape[0]
    width = w_ref.shape[1]
    dv = v_ref.shape[1]
    v0 = (n_qc + n_kc) * LANES

    @pl.when(i == 0)
    def _():
        acc_b[...] = jnp.zeros(acc_b.shape, F32)

    def step(mm_ref, ep_ref):
        for c0 in range(0, width, PROJ_COLS):
            cols = min(PROJ_COLS, width - c0)
            mm_ref[:, c0:c0 + cols] = _dot(h_ref[...], w_ref[:, c0:c0 + cols])
        cos, sin = cos_ref[...], sin_ref[...]
        for c in range(n_qc + n_kc):
            sl = slice(c * LANES, (c + 1) * LANES)
            y = _rms(ep_ref[:, sl], gain_ref[:, sl])
            y = y * cos + pltpu.roll(y, LANES // 2, 1) * sin
            if c < n_qc:
                g, r = divmod(c, per_group)
                yt = y.T.astype(BF16)
                for j in range(tm // tq):
                    q_ref[g, j, :, r * tq:(r + 1) * tq] = yt[:, j * tq:(j + 1) * tq]
            else:
                k_ref[c - n_qc] = y.astype(BF16)
        for g in range((width - v0) // dv):
            v_ref[g] = ep_ref[:, v0 + g * dv:v0 + (g + 1) * dv].T.astype(BF16)

    @pl.when(i % 2 == 0)
    def _():
        step(acc_a, acc_b)

    @pl.when(i % 2 == 1)
    def _():
        step(acc_b, acc_a)


def _proj(h, w, gain, cos, sin, *, bsz, seq, n_qc, n_kc, per_group, dv, tq, tk, tm):
    m, d = h.shape
    width = w.shape[1]
    qk_w = (n_qc + n_kc) * LANES
    groups, v_groups = n_qc // per_group, (width - qk_w) // dv
    nt, per_chunk, n = seq // tm, tk // tm, m // tm
    assert tm % tq == 0 and tk % tm == 0 and seq % tk == 0 and (width - qk_w) % dv == 0

    def cur(i):
        return jnp.minimum(i, n - 1)

    def prev(i):
        return jnp.maximum(i - 1, 0)

    return pl.pallas_call(
        functools.partial(_proj_kernel, n_qc=n_qc, n_kc=n_kc, per_group=per_group, tq=tq),
        grid=(n + 1,),
        in_specs=[
            pl.BlockSpec((tm, d), lambda i: (cur(i), 0)),
            pl.BlockSpec((d, width), lambda i: (0, 0)),
            pl.BlockSpec((1, qk_w), lambda i: (0, 0)),
            pl.BlockSpec((tm, LANES), lambda i: (prev(i) % nt, 0)),
            pl.BlockSpec((tm, LANES), lambda i: (prev(i) % nt, 0)),
        ],
        out_specs=[
            pl.BlockSpec((None, groups, tm // tq, LANES, per_group * tq),
                         lambda i: (prev(i) // nt, 0, prev(i) % nt, 0, 0)),
            pl.BlockSpec((None, n_kc, tm, LANES), lambda i: (prev(i) // nt, 0, prev(i) % nt, 0)),
            pl.BlockSpec((None, v_groups, None, dv, tm),
                         lambda i: (prev(i) // nt, 0, (prev(i) % nt) // per_chunk, 0, (prev(i) % nt) % per_chunk)),
        ],
        out_shape=[
            jax.ShapeDtypeStruct((bsz, groups, seq // tq, LANES, per_group * tq), BF16),
            jax.ShapeDtypeStruct((bsz, n_kc, seq, LANES), BF16),
            jax.ShapeDtypeStruct((bsz, v_groups, seq // tk, dv, tk), BF16),
        ],
        scratch_shapes=[pltpu.VMEM((tm, width), F32), pltpu.VMEM((tm, width), F32)],
        compiler_params=_params("arbitrary"),
        name="proj",
    )(h, w, gain, cos, sin)


def _attn_kernel(*refs, tq, diff, lambda_init):
    if diff:
        q_ref, k_ref, v_ref, lq1_ref, lk1_ref, lq2_ref, lk2_ref, subln_ref, o_ref, acc_ref, m_ref, l_ref, l8_ref, kmax_ref = refs
    else:
        q_ref, k_ref, v_ref, o_ref, acc_ref, m_ref, l_ref, l8_ref, kmax_ref = refs
    n_k, n_chunks, tk = k_ref.shape[0], k_ref.shape[1], k_ref.shape[2]
    w = q_ref.shape[1]
    wk = w // n_k

    def scores(c):
        parts = [_dot(k_ref[ks, c], q_ref[:, ks * wk:(ks + 1) * wk]) for ks in range(n_k)]
        return parts[0] if n_k == 1 else jnp.concatenate(parts, axis=1)

    @pl.when(pl.program_id(2) == 0)
    def _():
        for ks in range(n_k):
            def kbody(c, best):
                kf = k_ref[ks, c].astype(F32)
                return jnp.maximum(best, jnp.max(jnp.sum(kf * kf, axis=-1, keepdims=True), axis=0, keepdims=True))
            k2 = lax.fori_loop(0, n_chunks, kbody, jnp.zeros((1, 1), F32))
            kmax_ref[ks] = jnp.broadcast_to(jnp.sqrt(k2), (1, LANES))

    qf = q_ref[...].astype(F32)
    qn = jnp.sqrt(jnp.sum(qf * qf, axis=0, keepdims=True))
    for ks in range(n_k):
        m_ref[:, ks * wk:(ks + 1) * wk] = qn[:, ks * wk:(ks + 1) * wk] * kmax_ref[ks, :, 0:1]
    acc_ref[...] = jnp.zeros(acc_ref.shape, F32)
    l8_ref[...] = jnp.zeros(l8_ref.shape, F32)

    def fast_body(c, carry):
        p = jnp.exp2(scores(c) - m_ref[...])
        l8_ref[...] += jnp.sum(p.reshape(tk // 8, 8, w), axis=0)
        acc_ref[...] += _dot(v_ref[c], p.astype(BF16))
        return carry

    lax.fori_loop(0, n_chunks, fast_body, 0, unroll=True)
    l_ref[...] = jnp.sum(l8_ref[...], axis=0, keepdims=True)
    underflow = jnp.min(l_ref[...]) < L_MIN_SAFE

    @pl.when(underflow)
    def _():
        acc_ref[...] = jnp.zeros(acc_ref.shape, F32)
        m_ref[...] = jnp.full(m_ref.shape, NEG_BIG, F32)
        l_ref[...] = jnp.zeros(l_ref.shape, F32)

        def body(c, carry):
            s = scores(c)
            m_old = m_ref[...]
            m_new = jnp.maximum(m_old, jnp.max(s, axis=0, keepdims=True))
            alpha = jnp.exp2(m_old - m_new)
            p = jnp.exp2(s - m_new)
            l_ref[...] = alpha * l_ref[...] + jnp.sum(p, axis=0, keepdims=True)
            acc_ref[...] = alpha * acc_ref[...] + _dot(v_ref[c], p.astype(BF16))
            m_ref[...] = m_new
            return carry

        lax.fori_loop(0, n_chunks, body, 0)

    o = acc_ref[...] * (1.0 / l_ref[...])
    if not diff:
        for r in range(w // tq):
            o_ref[:, r * LANES:(r + 1) * LANES] = o[:, r * tq:(r + 1) * tq].T.astype(BF16)
    else:
        lam = (jnp.exp(jnp.sum(lq1_ref[...] * lk1_ref[...], axis=-1, keepdims=True))
               - jnp.exp(jnp.sum(lq2_ref[...] * lk2_ref[...], axis=-1, keepdims=True))
               + lambda_init)
        d = o[:, :tq] - lam * o[:, tq:]
        y = _rms(d.T, subln_ref[...]) * (1.0 - lambda_init)
        o_ref[...] = y.astype(BF16)


def _attn(qt, k, vt, *, tq, diff_args=None, lambda_init=0.0):
    b, g, nqt, hd, w = qt.shape
    n_k, nc, tk = k.shape[2], k.shape[3], k.shape[4]
    dv = vt.shape[3]
    diff = diff_args is not None
    width = dv if diff else (w // tq) * dv
    in_specs = [
        pl.BlockSpec((None, None, None, hd, w), lambda bi, gi, qi: (bi, gi, qi, 0, 0)),
        pl.BlockSpec((None, None, n_k, nc, tk, hd), lambda bi, gi, qi: (bi, gi, 0, 0, 0, 0),
                     pipeline_mode=pl.Buffered(1)),
        pl.BlockSpec((None, None, nc, dv, tk), lambda bi, gi, qi: (bi, gi, 0, 0, 0),
                     pipeline_mode=pl.Buffered(1)),
    ]
    args = [qt, k, vt]
    if diff:
        in_specs += [pl.BlockSpec((1, hd), lambda bi, gi, qi: (0, 0))] * 4
        in_specs += [pl.BlockSpec((1, dv), lambda bi, gi, qi: (0, 0))]
        args += list(diff_args)
    return pl.pallas_call(
        functools.partial(_attn_kernel, tq=tq, diff=diff, lambda_init=lambda_init),
        grid=(b, g, nqt),
        in_specs=in_specs,
        out_specs=pl.BlockSpec((None, tq, width), lambda bi, gi, qi: (bi, qi, gi)),
        out_shape=jax.ShapeDtypeStruct((b, nqt * tq, g * width), BF16),
        scratch_shapes=[
            pltpu.VMEM((dv, w), F32),
            pltpu.VMEM((1, w), F32),
            pltpu.VMEM((1, w), F32),
            pltpu.VMEM((8, w), F32),
            pltpu.VMEM((n_k, 1, LANES), F32),
        ],
        compiler_params=_params("parallel", "parallel", "arbitrary"),
        name="attn_diff" if diff else "attn_gqa",
    )(*args)


def _outproj_kernel(x_ref, a_ref, b_ref, wa_ref, wb_ref, o_ref):
    o_ref[...] = x_ref[...] + _dot(a_ref[...], wa_ref[...]) + _dot(b_ref[...], wb_ref[...])


def _outproj(x, a, b, w_out, *, tm):
    m, d = x.shape
    da, db = a.shape[1], b.shape[1]
    assert da == db and w_out.shape[0] == da + db
    return pl.pallas_call(
        _outproj_kernel,
        grid=(m // tm,),
        in_specs=[
            pl.BlockSpec((tm, d), lambda i: (i, 0)),
            pl.BlockSpec((tm, da), lambda i: (i, 0)),
            pl.BlockSpec((tm, db), lambda i: (i, 0)),
            pl.BlockSpec((da, d), lambda i: (0, 0)),
            pl.BlockSpec((db, d), lambda i: (1, 0)),
        ],
        out_specs=pl.BlockSpec((tm, d), lambda i: (i, 0)),
        out_shape=jax.ShapeDtypeStruct((m, d), F32),
        compiler_params=_params("parallel"),
        name="outproj",
    )(x, a, b, w_out, w_out)


def _rope_cs(pos, dim, theta):
    inv_freq = theta ** (-jnp.arange(0, dim, 2, dtype=F32) / dim)
    ang = pos[:, None] * inv_freq[None, :]
    return jnp.cos(ang), jnp.sin(ang)


ROPE_BLOCK = 16
AXIAL_ORDER = (0, 1, 4, 5, 2, 3, 6, 7)
PARTIAL_ORDER = (0, 2, 3, 4, 1, 5, 6, 7)


def _permute_heads(a, order):
    lead = a.shape[:-1]
    a4 = a.reshape(lead + (a.shape[-1] // HEAD_DIM, HEAD_DIM // ROPE_BLOCK, ROPE_BLOCK))
    return jnp.concatenate([a4[..., i:i + 1, :] for i in order], axis=-2).reshape(a.shape)


def _rope_tables(seq):
    rows = seq // GRID_W
    rc, rs = (jnp.repeat(a, GRID_W, axis=0) for a in _rope_cs(jnp.arange(rows, dtype=F32), AXIAL_DIM, A_ROPE_THETA))
    cc, cs = (jnp.tile(a, (rows, 1)) for a in _rope_cs(jnp.arange(GRID_W, dtype=F32), AXIAL_DIM, A_ROPE_THETA))
    pc, ps = _rope_cs(jnp.arange(seq, dtype=F32), PARTIAL_ROPE_DIM, PARTIAL_ROPE_THETA)
    cos_a = jnp.concatenate([rc, cc, rc, cc], axis=-1)
    sin_a = jnp.concatenate([-rs, -cs, rs, cs], axis=-1)
    rest = HEAD_DIM // 2 - PARTIAL_ROPE_DIM // 2
    one, zero = jnp.ones((seq, rest), F32), jnp.zeros((seq, rest), F32)
    cos_b = jnp.concatenate([pc, one, pc, one], axis=-1)
    sin_b = jnp.concatenate([-ps, zero, ps, zero], axis=-1)
    return (cos_a, sin_a), (cos_b, sin_b)


def _tiles(m, seq, d_ff):
    def fit(n, t):
        t = min(t, n)
        while n % t:
            t //= 2
        return t
    return dict(
        ffn1_tm=fit(m, 1024), ffn2_tm=fit(m, 1024), ffn_tf=fit(d_ff, 512),
        proj_tm=fit(seq, 512), out_tm=fit(m, 512),
        a_tq=fit(seq, 256), b_tq=fit(seq, 512), a_tk=fit(seq, 4096), b_tk=fit(seq, 4096),
    )


def kernel(x, ffn1_norm, ffn1_w_gu, ffn1_w_down, mix_norm, w_in, a_q_norm, a_k_norm, b_q_norm, b_k_norm,
           b_lambda_q1, b_lambda_k1, b_lambda_q2, b_lambda_k2, b_subln, w_out, ffn2_norm, ffn2_w_gu,
           ffn2_w_down, out_norm):
    bsz, seq, d = x.shape
    depth = w_in.shape[0]
    d_ff = ffn1_w_down.shape[1]
    m = bsz * seq
    hd = HEAD_DIM
    a_heads = d // (2 * hd)
    a_kv = a_heads // A_GROUP
    b_vdim = 2 * hd
    b_heads = d // (2 * b_vdim)
    a_q, a_kvw = a_heads * hd, a_kv * hd
    b_qk, b_v = b_heads * 2 * hd, b_heads * b_vdim
    t = _tiles(m, seq, d_ff)
    q_scale = (hd ** -0.5) * LOG2E
    tab_a, tab_b = _rope_tables(seq)

    x = x.reshape(m, d)
    for l in range(depth):
        lambda_init = 0.8 - 0.6 * math.exp(-0.3 * l)
        row = lambda v: v[l].reshape(1, -1).astype(F32)
        w_gu1, w_d1 = ffn1_w_gu[l].astype(BF16), ffn1_w_down[l].astype(BF16)
        w_gu2, w_d2 = ffn2_w_gu[l].astype(BF16), ffn2_w_down[l].astype(BF16)
        w_o = w_out[l].astype(BF16)
        c1 = a_q + a_kvw
        c2 = c1 + a_kvw
        c3 = c2 + 2 * b_qk
        w_i = w_in[l]
        w_a = jnp.concatenate([_permute_heads(w_i[:, :c1], AXIAL_ORDER), w_i[:, c1:c2]], axis=1).astype(BF16)
        w_b = jnp.concatenate([_permute_heads(w_i[:, c2:c3], PARTIAL_ORDER), w_i[:, c3:]], axis=1).astype(BF16)

        x, h = _ffn(x, row(ffn1_norm), w_gu1, w_d1, row(mix_norm),
                    emit_norm=True, final_norm=False, tm=t["ffn1_tm"], tf=t["ffn_tf"])

        tqa, tqb, tka, tkb, ptm = t["a_tq"], t["b_tq"], t["a_tk"], t["b_tk"], t["proj_tm"]
        gain_a = jnp.concatenate([jnp.tile(_permute_heads(row(a_q_norm), AXIAL_ORDER) * q_scale, (1, a_heads)),
                                  jnp.tile(_permute_heads(row(a_k_norm), AXIAL_ORDER), (1, a_kv))], axis=-1)
        qa_t, ka, va_t = _proj(h, w_a, gain_a, *tab_a, bsz=bsz, seq=seq, n_qc=a_heads, n_kc=a_kv,
                               per_group=A_GROUP, dv=hd, tq=tqa, tk=tka, tm=ptm)
        gain_b = jnp.concatenate([jnp.tile(_permute_heads(row(b_q_norm), PARTIAL_ORDER) * q_scale, (1, 2 * b_heads)),
                                  jnp.tile(_permute_heads(row(b_k_norm), PARTIAL_ORDER), (1, 2 * b_heads))], axis=-1)
        qb_t, kb, vb_t = _proj(h, w_b, gain_b, *tab_b, bsz=bsz, seq=seq, n_qc=2 * b_heads, n_kc=2 * b_heads,
                               per_group=2, dv=b_vdim, tq=tqb, tk=tkb, tm=ptm)
        ka = ka.reshape(bsz, a_kv, 1, seq // tka, tka, hd)
        kb = kb.reshape(bsz, b_heads, 2, seq // tkb, tkb, hd)

        out_a = _attn(qa_t, ka, va_t, tq=tqa)
        out_b = _attn(qb_t, kb, vb_t, tq=tqb,
                      diff_args=(row(b_lambda_q1), row(b_lambda_k1), row(b_lambda_q2), row(b_lambda_k2), row(b_subln)),
                      lambda_init=lambda_init)

        x = _outproj(x, out_a.reshape(m, a_q), out_b.reshape(m, b_v), w_o, tm=t["out_tm"])
        x, = _ffn(x, row(ffn2_norm), w_gu2, w_d2, row(out_norm),
                  emit_norm=False, final_norm=True, tm=t["ffn2_tm"], tf=t["ffn_tf"])
    return x.reshape(bsz, seq, d)
```

```python
import functools
import math

import jax
import jax.numpy as jnp
from jax import lax
from jax.experimental import pallas as pl
from jax.experimental.pallas import tpu as pltpu

HEAD_DIM = 128
GRID_W = 64
EPS = 1e-6
A_ROPE_THETA = 10000.0
PARTIAL_ROPE_THETA = 500000.0
AXIAL_DIM = HEAD_DIM // 2
PARTIAL_ROPE_DIM = HEAD_DIM // 4
A_GROUP = 4
LANES = 128
LOG2E = math.log2(math.e)
VMEM_LIMIT_BYTES = 63 * 1024 * 1024
NEG_BIG = -1e30
L_MIN_SAFE = 1e-20

BF16 = jnp.bfloat16
F32 = jnp.float32


def _dot(a, b):
    return jnp.dot(a, b, preferred_element_type=F32)


def _rms(x, gain):
    r = lax.rsqrt(jnp.mean(x * x, axis=-1, keepdims=True) + EPS)
    return (x * r) * gain


def _params(*semantics):
    return pltpu.CompilerParams(dimension_semantics=semantics, vmem_limit_bytes=VMEM_LIMIT_BYTES)


def _ffn_kernel(x_ref, gain_ref, wg_ref, wu_ref, wd_ref, ngain_ref, o_ref, *rest, emit_norm, final_norm, single_step):
    if emit_norm:
        hn_ref, xn_ref = rest
    else:
        (xn_ref,) = rest
    j = pl.program_id(1)
    last = pl.num_programs(1) - 1

    def step(first, final):
        if first:
            x = x_ref[...]
            xn = _rms(x, gain_ref[...]).astype(BF16)
            xn_ref[...] = xn
        else:
            xn = xn_ref[...]
        g = _dot(xn, wg_ref[...])
        u = _dot(xn, wu_ref[...])
        act = (0.5 * g / (1.0 + jnp.exp(-g))) * u
        o = (x_ref[...] if first else o_ref[...]) + _dot(act.astype(BF16), wd_ref[...])
        if final:
            y = _rms(o, ngain_ref[...])
            if emit_norm:
                hn_ref[...] = y.astype(BF16)
            if final_norm:
                o = y
        o_ref[...] = o

    if single_step:
        step(True, True)
    else:
        pl.when(j == 0)(lambda: step(True, False))
        pl.when(jnp.logical_and(j > 0, j < last))(lambda: step(False, False))
        pl.when(j == last)(lambda: step(False, True))


def _ffn(x, gain, w_gu, w_down, next_gain, *, emit_norm, final_norm, tm, tf):
    m, d = x.shape
    d_ff = w_down.shape[0]
    nf = d_ff // tf
    assert m % tm == 0 and d_ff % tf == 0
    out_shape = [jax.ShapeDtypeStruct((m, d), F32)]
    out_specs = [pl.BlockSpec((tm, d), lambda i, j: (i, 0))]
    if emit_norm:
        out_shape.append(jax.ShapeDtypeStruct((m, d), BF16))
        out_specs.append(pl.BlockSpec((tm, d), lambda i, j: (i, 0)))
    res = pl.pallas_call(
        functools.partial(_ffn_kernel, emit_norm=emit_norm, final_norm=final_norm, single_step=nf == 1),
        grid=(m // tm, nf),
        in_specs=[
            pl.BlockSpec((tm, d), lambda i, j: (i, 0)),
            pl.BlockSpec((1, d), lambda i, j: (0, 0)),
            pl.BlockSpec((d, tf), lambda i, j: (0, j)),
            pl.BlockSpec((d, tf), lambda i, j: (0, j + nf)),
            pl.BlockSpec((tf, d), lambda i, j: (j, 0)),
            pl.BlockSpec((1, d), lambda i, j: (0, 0)),
        ],
        out_specs=out_specs,
        out_shape=out_shape,
        scratch_shapes=[pltpu.VMEM((tm, d), BF16)],
        compiler_params=_params("parallel", "arbitrary"),
        name="ffn",
    )(x, gain, w_gu, w_gu, w_down, next_gain)
    return res


PROJ_COLS = 2 * LANES


def _proj_kernel(h_ref, w_ref, gain_ref, cos_ref, sin_ref, q_ref, k_ref, v_ref, acc_a, acc_b, *,
                 n_qc, n_kc, per_group, tq):
    i = pl.program_id(0)
    tm = h_ref.shape[0]
    width = w_ref.shape[1]
    dv = v_ref.shape[1]
    v0 = (n_qc + n_kc) * LANES

    @pl.when(i == 0)
    def _():
        acc_b[...] = jnp.zeros(acc_b.shape, F32)

    def step(mm_ref, ep_ref):
        for c0 in range(0, width, PROJ_COLS):
            cols = min(PROJ_COLS, width - c0)
            mm_ref[:, c0:c0 + cols] = _dot(h_ref[...], w_ref[:, c0:c0 + cols])
        cos, sin = cos_ref[...], sin_ref[...]
        for c in range(n_qc + n_kc):
            sl = slice(c * LANES, (c + 1) * LANES)
            y = _rms(ep_ref[:, sl], gain_ref[:, sl])
            y = y * cos + pltpu.roll(y, LANES // 2, 1) * sin
            if c < n_qc:
                g, r = divmod(c, per_group)
                yt = y.T.astype(BF16)
                for j in range(tm // tq):
                    q_ref[g, j, :, r * tq:(r + 1) * tq] = yt[:, j * tq:(j + 1) * tq]
            else:
                k_ref[c - n_qc] = y.astype(BF16)
        for g in range((width - v0) // dv):
            v_ref[g] = ep_ref[:, v0 + g * dv:v0 + (g + 1) * dv].T.astype(BF16)

    @pl.when(i % 2 == 0)
    def _():
        step(acc_a, acc_b)

    @pl.when(i % 2 == 1)
    def _():
        step(acc_b, acc_a)


def _proj(h, w, gain, cos, sin, *, bsz, seq, n_qc, n_kc, per_group, dv, tq, tk, tm):
    m, d = h.shape
    width = w.shape[1]
    qk_w = (n_qc + n_kc) * LANES
    groups, v_groups = n_qc // per_group, (width - qk_w) // dv
    nt, per_chunk, n = seq // tm, tk // tm, m // tm
    assert tm % tq == 0 and tk % tm == 0 and seq % tk == 0 and (width - qk_w) % dv == 0

    def cur(i):
        return jnp.minimum(i, n - 1)

    def prev(i):
        return jnp.maximum(i - 1, 0)

    return pl.pallas_call(
        functools.partial(_proj_kernel, n_qc=n_qc, n_kc=n_kc, per_group=per_group, tq=tq),
        grid=(n + 1,),
        in_specs=[
            pl.BlockSpec((tm, d), lambda i: (cur(i), 0)),
            pl.BlockSpec((d, width), lambda i: (0, 0)),
            pl.BlockSpec((1, qk_w), lambda i: (0, 0)),
            pl.BlockSpec((tm, LANES), lambda i: (prev(i) % nt, 0)),
            pl.BlockSpec((tm, LANES), lambda i: (prev(i) % nt, 0)),
        ],
        out_specs=[
            pl.BlockSpec((None, groups, tm // tq, LANES, per_group * tq),
                         lambda i: (prev(i) // nt, 0, prev(i) % nt, 0, 0)),
            pl.BlockSpec((None, n_kc, tm, LANES), lambda i: (prev(i) // nt, 0, prev(i) % nt, 0)),
            pl.BlockSpec((None, v_groups, None, dv, tm),
                         lambda i: (prev(i) // nt, 0, (prev(i) % nt) // per_chunk, 0, (prev(i) % nt) % per_chunk)),
        ],
        out_shape=[
            jax.ShapeDtypeStruct((bsz, groups, seq // tq, LANES, per_group * tq), BF16),
            jax.ShapeDtypeStruct((bsz, n_kc, seq, LANES), BF16),
            jax.ShapeDtypeStruct((bsz, v_groups, seq // tk, dv, tk), BF16),
        ],
        scratch_shapes=[pltpu.VMEM((tm, width), F32), pltpu.VMEM((tm, width), F32)],
        compiler_params=_params("arbitrary"),
        name="proj",
    )(h, w, gain, cos, sin)


def _attn_kernel(*refs, tq, diff, lambda_init):
    if diff:
        q_ref, k_ref, v_ref, lq1_ref, lk1_ref, lq2_ref, lk2_ref, subln_ref, o_ref, acc_ref, m_ref, l_ref, l8_ref, kmax_ref = refs
    else:
        q_ref, k_ref, v_ref, o_ref, acc_ref, m_ref, l_ref, l8_ref, kmax_ref = refs
    n_k, n_chunks, tk = k_ref.shape[0], k_ref.shape[1], k_ref.shape[2]
    w = q_ref.shape[1]
    wk = w // n_k

    def scores(c):
        parts = [_dot(k_ref[ks, c], q_ref[:, ks * wk:(ks + 1) * wk]) for ks in range(n_k)]
        return parts[0] if n_k == 1 else jnp.concatenate(parts, axis=1)

    @pl.when(pl.program_id(2) == 0)
    def _():
        for ks in range(n_k):
            def kbody(c, best):
                kf = k_ref[ks, c].astype(F32)
                return jnp.maximum(best, jnp.max(jnp.sum(kf * kf, axis=-1, keepdims=True), axis=0, keepdims=True))
            k2 = lax.fori_loop(0, n_chunks, kbody, jnp.zeros((1, 1), F32))
            kmax_ref[ks] = jnp.broadcast_to(jnp.sqrt(k2), (1, LANES))

    qf = q_ref[...].astype(F32)
    qn = jnp.sqrt(jnp.sum(qf * qf, axis=0, keepdims=True))
    for ks in range(n_k):
        m_ref[:, ks * wk:(ks + 1) * wk] = qn[:, ks * wk:(ks + 1) * wk] * kmax_ref[ks, :, 0:1]
    acc_ref[...] = jnp.zeros(acc_ref.shape, F32)
    l8_ref[...] = jnp.zeros(l8_ref.shape, F32)

    def fast_body(c, carry):
        p = jnp.exp2(scores(c) - m_ref[...])
        l8_ref[...] += jnp.sum(p.reshape(tk // 8, 8, w), axis=0)
        acc_ref[...] += _dot(v_ref[c], p.astype(BF16))
        return carry

    lax.fori_loop(0, n_chunks, fast_body, 0, unroll=True)
    l_ref[...] = jnp.sum(l8_ref[...], axis=0, keepdims=True)
    underflow = jnp.min(l_ref[...]) < L_MIN_SAFE

    @pl.when(underflow)
    def _():
        acc_ref[...] = jnp.zeros(acc_ref.shape, F32)
        m_ref[...] = jnp.full(m_ref.shape, NEG_BIG, F32)
        l_ref[...] = jnp.zeros(l_ref.shape, F32)

        def body(c, carry):
            s = scores(c)
            m_old = m_ref[...]
            m_new = jnp.maximum(m_old, jnp.max(s, axis=0, keepdims=True))
            alpha = jnp.exp2(m_old - m_new)
            p = jnp.exp2(s - m_new)
            l_ref[...] = alpha * l_ref[...] + jnp.sum(p, axis=0, keepdims=True)
            acc_ref[...] = alpha * acc_ref[...] + _dot(v_ref[c], p.astype(BF16))
            m_ref[...] = m_new
            return carry

        lax.fori_loop(0, n_chunks, body, 0)

    o = acc_ref[...] * (1.0 / l_ref[...])
    if not diff:
        for r in range(w // tq):
            o_ref[:, r * LANES:(r + 1) * LANES] = o[:, r * tq:(r + 1) * tq].T.astype(BF16)
    else:
        lam = (jnp.exp(jnp.sum(lq1_ref[...] * lk1_ref[...], axis=-1, keepdims=True))
               - jnp.exp(jnp.sum(lq2_ref[...] * lk2_ref[...], axis=-1, keepdims=True))
               + lambda_init)
        d = o[:, :tq] - lam * o[:, tq:]
        y = _rms(d.T, subln_ref[...]) * (1.0 - lambda_init)
        o_ref[...] = y.astype(BF16)


def _attn(qt, k, vt, *, tq, diff_args=None, lambda_init=0.0):
    b, g, nqt, hd, w = qt.shape
    n_k, nc, tk = k.shape[2], k.shape[3], k.shape[4]
    dv = vt.shape[3]
    diff = diff_args is not None
    width = dv if diff else (w // tq) * dv
    in_specs = [
        pl.BlockSpec((None, None, None, hd, w), lambda bi, gi, qi: (bi, gi, qi, 0, 0)),
        pl.BlockSpec((None, None, n_k, nc, tk, hd), lambda bi, gi, qi: (bi, gi, 0, 0, 0, 0),
                     pipeline_mode=pl.Buffered(1)),
        pl.BlockSpec((None, None, nc, dv, tk), lambda bi, gi, qi: (bi, gi, 0, 0, 0),
                     pipeline_mode=pl.Buffered(1)),
    ]
    args = [qt, k, vt]
    if diff:
        in_specs += [pl.BlockSpec((1, hd), lambda bi, gi, qi: (0, 0))] * 4
        in_specs += [pl.BlockSpec((1, dv), lambda bi, gi, qi: (0, 0))]
        args += list(diff_args)
    return pl.pallas_call(
        functools.partial(_attn_kernel, tq=tq, diff=diff, lambda_init=lambda_init),
        grid=(b, g, nqt),
        in_specs=in_specs,
        out_specs=pl.BlockSpec((None, tq, width), lambda bi, gi, qi: (bi, qi, gi)),
        out_shape=jax.ShapeDtypeStruct((b, nqt * tq, g * width), BF16),
        scratch_shapes=[
            pltpu.VMEM((dv, w), F32),
            pltpu.VMEM((1, w), F32),
            pltpu.VMEM((1, w), F32),
            pltpu.VMEM((8, w), F32),
            pltpu.VMEM((n_k, 1, LANES), F32),
        ],
        compiler_params=_params("parallel", "parallel", "arbitrary"),
        name="attn_diff" if diff else "attn_gqa",
    )(*args)


def _outproj_kernel(x_ref, a_ref, b_ref, wa_ref, wb_ref, o_ref):
    o_ref[...] = x_ref[...] + _dot(a_ref[...], wa_ref[...]) + _dot(b_ref[...], wb_ref[...])


def _outproj(x, a, b, w_out, *, tm):
    m, d = x.shape
    da, db = a.shape[1], b.shape[1]
    assert da == db and w_out.shape[0] == da + db
    return pl.pallas_call(
        _outproj_kernel,
        grid=(m // tm,),
        in_specs=[
            pl.BlockSpec((tm, d), lambda i: (i, 0)),
            pl.BlockSpec((tm, da), lambda i: (i, 0)),
            pl.BlockSpec((tm, db), lambda i: (i, 0)),
            pl.BlockSpec((da, d), lambda i: (0, 0)),
            pl.BlockSpec((db, d), lambda i: (1, 0)),
        ],
        out_specs=pl.BlockSpec((tm, d), lambda i: (i, 0)),
        out_shape=jax.ShapeDtypeStruct((m, d), F32),
        compiler_params=_params("parallel"),
        name="outproj",
    )(x, a, b, w_out, w_out)


def _rope_cs(pos, dim, theta):
    inv_freq = theta ** (-jnp.arange(0, dim, 2, dtype=F32) / dim)
    ang = pos[:, None] * inv_freq[None, :]
    return jnp.cos(ang), jnp.sin(ang)


ROPE_BLOCK = 16
AXIAL_ORDER = (0, 1, 4, 5, 2, 3, 6, 7)
PARTIAL_ORDER = (0, 2, 3, 4, 1, 5, 6, 7)


def _permute_heads(a, order):
    lead = a.shape[:-1]
    a4 = a.reshape(lead + (a.shape[-1] // HEAD_DIM, HEAD_DIM // ROPE_BLOCK, ROPE_BLOCK))
    return jnp.concatenate([a4[..., i:i + 1, :] for i in order], axis=-2).reshape(a.shape)


def _rope_tables(seq):
    rows = seq // GRID_W
    rc, rs = (jnp.repeat(a, GRID_W, axis=0) for a in _rope_cs(jnp.arange(rows, dtype=F32), AXIAL_DIM, A_ROPE_THETA))
    cc, cs = (jnp.tile(a, (rows, 1)) for a in _rope_cs(jnp.arange(GRID_W, dtype=F32), AXIAL_DIM, A_ROPE_THETA))
    pc, ps = _rope_cs(jnp.arange(seq, dtype=F32), PARTIAL_ROPE_DIM, PARTIAL_ROPE_THETA)
    cos_a = jnp.concatenate([rc, cc, rc, cc], axis=-1)
    sin_a = jnp.concatenate([-rs, -cs, rs, cs], axis=-1)
    rest = HEAD_DIM // 2 - PARTIAL_ROPE_DIM // 2
    one, zero = jnp.ones((seq, rest), F32), jnp.zeros((seq, rest), F32)
    cos_b = jnp.concatenate([pc, one, pc, one], axis=-1)
    sin_b = jnp.concatenate([-ps, zero, ps, zero], axis=-1)
    return (cos_a, sin_a), (cos_b, sin_b)


def _tiles(m, seq, d_ff):
    def fit(n, t):
        t = min(t, n)
        while n % t:
            t //= 2
        return t
    return dict(
        ffn1_tm=fit(m, 1024), ffn2_tm=fit(m, 1024), ffn_tf=fit(d_ff, 512),
        proj_tm=fit(seq, 512), out_tm=fit(m, 512),
        a_tq=fit(seq, 256), b_tq=fit(seq, 512), a_tk=fit(seq, 4096), b_tk=fit(seq, 4096),
    )


def kernel(x, ffn1_norm, ffn1_w_gu, ffn1_w_down, mix_norm, w_in, a_q_norm, a_k_norm, b_q_norm, b_k_norm,
           b_lambda_q1, b_lambda_k1, b_lambda_q2, b_lambda_k2, b_subln, w_out, ffn2_norm, ffn2_w_gu,
           ffn2_w_down, out_norm):
    bsz, seq, d = x.shape
    depth = w_in.shape[0]
    d_ff = ffn1_w_down.shape[1]
    m = bsz * seq
    hd = HEAD_DIM
    a_heads = d // (2 * hd)
    a_kv = a_heads // A_GROUP
    b_vdim = 2 * hd
    b_heads = d // (2 * b_vdim)
    a_q, a_kvw = a_heads * hd, a_kv * hd
    b_qk, b_v = b_heads * 2 * hd, b_heads * b_vdim
    t = _tiles(m, seq, d_ff)
    q_scale = (hd ** -0.5) * LOG2E
    tab_a, tab_b = _rope_tables(seq)

    x = x.reshape(m, d)
    for l in range(depth):
        lambda_init = 0.8 - 0.6 * math.exp(-0.3 * l)
        row = lambda v: v[l].reshape(1, -1).astype(F32)
        w_gu1, w_d1 = ffn1_w_gu[l].astype(BF16), ffn1_w_down[l].astype(BF16)
        w_gu2, w_d2 = ffn2_w_gu[l].astype(BF16), ffn2_w_down[l].astype(BF16)
        w_o = w_out[l].astype(BF16)
        c1 = a_q + a_kvw
        c2 = c1 + a_kvw
        c3 = c2 + 2 * b_qk
        w_i = w_in[l]
        w_a = jnp.concatenate([_permute_heads(w_i[:, :c1], AXIAL_ORDER), w_i[:, c1:c2]], axis=1).astype(BF16)
        w_b = jnp.concatenate([_permute_heads(w_i[:, c2:c3], PARTIAL_ORDER), w_i[:, c3:]], axis=1).astype(BF16)

        x, h = _ffn(x, row(ffn1_norm), w_gu1, w_d1, row(mix_norm),
                    emit_norm=True, final_norm=False, tm=t["ffn1_tm"], tf=t["ffn_tf"])

        tqa, tqb, tka, tkb, ptm = t["a_tq"], t["b_tq"], t["a_tk"], t["b_tk"], t["proj_tm"]
        gain_a = jnp.concatenate([jnp.tile(_permute_heads(row(a_q_norm), AXIAL_ORDER) * q_scale, (1, a_heads)),
                                  jnp.tile(_permute_heads(row(a_k_norm), AXIAL_ORDER), (1, a_kv))], axis=-1)
        qa_t, ka, va_t = _proj(h, w_a, gain_a, *tab_a, bsz=bsz, seq=seq, n_qc=a_heads, n_kc=a_kv,
                               per_group=A_GROUP, dv=hd, tq=tqa, tk=tka, tm=ptm)
        gain_b = jnp.concatenate([jnp.tile(_permute_heads(row(b_q_norm), PARTIAL_ORDER) * q_scale, (1, 2 * b_heads)),
                                  jnp.tile(_permute_heads(row(b_k_norm), PARTIAL_ORDER), (1, 2 * b_heads))], axis=-1)
        qb_t, kb, vb_t = _proj(h, w_b, gain_b, *tab_b, bsz=bsz, seq=seq, n_qc=2 * b_heads, n_kc=2 * b_heads,
                               per_group=2, dv=b_vdim, tq=tqb, tk=tkb, tm=ptm)
        ka = ka.reshape(bsz, a_kv, 1, seq // tka, tka, hd)
        kb = kb.reshape(bsz, b_heads, 2, seq // tkb, tkb, hd)

        out_a = _attn(qa_t, ka, va_t, tq=tqa)
        out_b = _attn(qb_t, kb, vb_t, tq=tqb,
                      diff_args=(row(b_lambda_q1), row(b_lambda_k1), row(b_lambda_q2), row(b_lambda_k2), row(b_subln)),
                      lambda_init=lambda_init)

        x = _outproj(x, out_a.reshape(m, a_q), out_b.reshape(m, b_v), w_o, tm=t["out_tm"])
        x, = _ffn(x, row(ffn2_norm), w_gu2, w_d2, row(out_norm),
                  emit_norm=False, final_norm=True, tm=t["ffn2_tm"], tf=t["ffn_tf"])
    return x.reshape(bsz, seq, d)
```

```python
import functools
import math

import jax
import jax.numpy as jnp
from jax import lax
from jax.experimental import pallas as pl
from jax.experimental.pallas import tpu as pltpu

HEAD_DIM = 128
GRID_W = 64
EPS = 1e-6
A_ROPE_THETA = 10000.0
PARTIAL_ROPE_THETA = 500000.0
AXIAL_DIM = HEAD_DIM // 2
PARTIAL_ROPE_DIM = HEAD_DIM // 4
A_GROUP = 4
LANES = 128
LOG2E = math.log2(math.e)
VMEM_LIMIT_BYTES = 63 * 1024 * 1024
NEG_BIG = -1e30
L_MIN_SAFE = 1e-20

BF16 = jnp.bfloat16
F32 = jnp.float32


def _dot(a, b):
    return jnp.dot(a, b, preferred_element_type=F32)


def _rms(x, gain):
    r = lax.rsqrt(jnp.mean(x * x, axis=-1, keepdims=True) + EPS)
    return (x * r) * gain


def _params(*semantics):
    return pltpu.CompilerParams(dimension_semantics=semantics, vmem_limit_bytes=VMEM_LIMIT_BYTES)


def _ffn_kernel(x_ref, gain_ref, wg_ref, wu_ref, wd_ref, ngain_ref, o_ref, *rest, emit_norm, final_norm, single_step):
    if emit_norm:
        hn_ref, xn_ref = rest
    else:
        (xn_ref,) = rest
    j = pl.program_id(1)
    last = pl.num_programs(1) - 1

    def step(first, final):
        if first:
            x = x_ref[...]
            xn = _rms(x, gain_ref[...]).astype(BF16)
            xn_ref[...] = xn
        else:
            xn = xn_ref[...]
        g = _dot(xn, wg_ref[...])
        u = _dot(xn, wu_ref[...])
        act = (0.5 * g / (1.0 + jnp.exp(-g))) * u
        o = (x_ref[...] if first else o_ref[...]) + _dot(act.astype(BF16), wd_ref[...])
        if final:
            y = _rms(o, ngain_ref[...])
            if emit_norm:
                hn_ref[...] = y.astype(BF16)
            if final_norm:
                o = y
        o_ref[...] = o

    if single_step:
        step(True, True)
    else:
        pl.when(j == 0)(lambda: step(True, False))
        pl.when(jnp.logical_and(j > 0, j < last))(lambda: step(False, False))
        pl.when(j == last)(lambda: step(False, True))


def _ffn(x, gain, w_gu, w_down, next_gain, *, emit_norm, final_norm, tm, tf):
    m, d = x.shape
    d_ff = w_down.shape[0]
    nf = d_ff // tf
    assert m % tm == 0 and d_ff % tf == 0
    out_shape = [jax.ShapeDtypeStruct((m, d), F32)]
    out_specs = [pl.BlockSpec((tm, d), lambda i, j: (i, 0))]
    if emit_norm:
        out_shape.append(jax.ShapeDtypeStruct((m, d), BF16))
        out_specs.append(pl.BlockSpec((tm, d), lambda i, j: (i, 0)))
    res = pl.pallas_call(
        functools.partial(_ffn_kernel, emit_norm=emit_norm, final_norm=final_norm, single_step=nf == 1),
        grid=(m // tm, nf),
        in_specs=[
            pl.BlockSpec((tm, d), lambda i, j: (i, 0)),
            pl.BlockSpec((1, d), lambda i, j: (0, 0)),
            pl.BlockSpec((d, tf), lambda i, j: (0, j)),
            pl.BlockSpec((d, tf), lambda i, j: (0, j + nf)),
            pl.BlockSpec((tf, d), lambda i, j: (j, 0)),
            pl.BlockSpec((1, d), lambda i, j: (0, 0)),
        ],
        out_specs=out_specs,
        out_shape=out_shape,
        scratch_shapes=[pltpu.VMEM((tm, d), BF16)],
        compiler_params=_params("parallel", "arbitrary"),
        name="ffn",
    )(x, gain, w_gu, w_gu, w_down, next_gain)
    return res


PROJ_COLS = 2 * LANES


def _proj_kernel(h_ref, wqk_ref, wv_ref, gain_ref, cos_ref, sin_ref, q_ref, k_ref, v_ref, acc_a, acc_b, *,
                 n_qc, n_kc, per_group, tq):
    i = pl.program_id(0)
    tm = h_ref.shape[0]
    dv = v_ref.shape[1]
    v0 = wqk_ref.shape[1]
    width = v0 + wv_ref.shape[1]

    @pl.when(i == 0)
    def _():
        acc_b[...] = jnp.zeros(acc_b.shape, F32)

    def step(mm_ref, ep_ref):
        for w_ref, base in ((wqk_ref, 0), (wv_ref, v0)):
            for c0 in range(0, w_ref.shape[1], PROJ_COLS):
                cols = min(PROJ_COLS, w_ref.shape[1] - c0)
                mm_ref[:, base + c0:base + c0 + cols] = _dot(h_ref[...], w_ref[:, c0:c0 + cols])
        cos, sin = cos_ref[...], sin_ref[...]
        for c in range(n_qc + n_kc):
            sl = slice(c * LANES, (c + 1) * LANES)
            y = _rms(ep_ref[:, sl], gain_ref[:, sl])
            y = y * cos + pltpu.roll(y, LANES // 2, 1) * sin
            if c < n_qc:
                g, r = divmod(c, per_group)
                yt = y.T.astype(BF16)
                for j in range(tm // tq):
                    q_ref[g, j, :, r * tq:(r + 1) * tq] = yt[:, j * tq:(j + 1) * tq]
            else:
                k_ref[c - n_qc] = y.astype(BF16)
        for g in range((width - v0) // dv):
            v_ref[g] = ep_ref[:, v0 + g * dv:v0 + (g + 1) * dv].T.astype(BF16)

    @pl.when(i % 2 == 0)
    def _():
        step(acc_a, acc_b)

    @pl.when(i % 2 == 1)
    def _():
        step(acc_b, acc_a)


def _proj(h, w_qk, w_v, gain, cos, sin, *, bsz, seq, n_qc, n_kc, per_group, dv, tq, tk, tm):
    m, d = h.shape
    qk_w, v_w = w_qk.shape[1], w_v.shape[1]
    width = qk_w + v_w
    groups, v_groups = n_qc // per_group, v_w // dv
    nt, per_chunk, n = seq // tm, tk // tm, m // tm
    assert tm % tq == 0 and tk % tm == 0 and seq % tk == 0 and v_w % dv == 0 and qk_w == (n_qc + n_kc) * LANES

    def cur(i):
        return jnp.minimum(i, n - 1)

    def prev(i):
        return jnp.maximum(i - 1, 0)

    return pl.pallas_call(
        functools.partial(_proj_kernel, n_qc=n_qc, n_kc=n_kc, per_group=per_group, tq=tq),
        grid=(n + 1,),
        in_specs=[
            pl.BlockSpec((tm, d), lambda i: (cur(i), 0)),
            pl.BlockSpec((d, qk_w), lambda i: (0, 0)),
            pl.BlockSpec((d, v_w), lambda i: (0, 0)),
            pl.BlockSpec((1, qk_w), lambda i: (0, 0)),
            pl.BlockSpec((tm, LANES), lambda i: (prev(i) % nt, 0)),
            pl.BlockSpec((tm, LANES), lambda i: (prev(i) % nt, 0)),
        ],
        out_specs=[
            pl.BlockSpec((None, groups, tm // tq, LANES, per_group * tq),
                         lambda i: (prev(i) // nt, 0, prev(i) % nt, 0, 0)),
            pl.BlockSpec((None, n_kc, tm, LANES), lambda i: (prev(i) // nt, 0, prev(i) % nt, 0)),
            pl.BlockSpec((None, v_groups, None, dv, tm),
                         lambda i: (prev(i) // nt, 0, (prev(i) % nt) // per_chunk, 0, (prev(i) % nt) % per_chunk)),
        ],
        out_shape=[
            jax.ShapeDtypeStruct((bsz, groups, seq // tq, LANES, per_group * tq), BF16),
            jax.ShapeDtypeStruct((bsz, n_kc, seq, LANES), BF16),
            jax.ShapeDtypeStruct((bsz, v_groups, seq // tk, dv, tk), BF16),
        ],
        scratch_shapes=[pltpu.VMEM((tm, width), F32), pltpu.VMEM((tm, width), F32)],
        compiler_params=_params("arbitrary"),
        name="proj",
    )(h, w_qk, w_v, gain, cos, sin)


def _attn_kernel(*refs, tq, diff, lambda_init):
    if diff:
        q_ref, k_ref, v_ref, lq1_ref, lk1_ref, lq2_ref, lk2_ref, subln_ref, o_ref, acc_ref, m_ref, l_ref, l8_ref, kmax_ref = refs
    else:
        q_ref, k_ref, v_ref, o_ref, acc_ref, m_ref, l_ref, l8_ref, kmax_ref = refs
    n_k, n_chunks, tk = k_ref.shape[0], k_ref.shape[1], k_ref.shape[2]
    w = q_ref.shape[1]
    wk = w // n_k

    def scores(c):
        parts = [_dot(k_ref[ks, c], q_ref[:, ks * wk:(ks + 1) * wk]) for ks in range(n_k)]
        return parts[0] if n_k == 1 else jnp.concatenate(parts, axis=1)

    @pl.when(pl.program_id(2) == 0)
    def _():
        for ks in range(n_k):
            def kbody(c, best):
                kf = k_ref[ks, c].astype(F32)
                return jnp.maximum(best, jnp.max(jnp.sum(kf * kf, axis=-1, keepdims=True), axis=0, keepdims=True))
            k2 = lax.fori_loop(0, n_chunks, kbody, jnp.zeros((1, 1), F32))
            kmax_ref[ks] = jnp.broadcast_to(jnp.sqrt(k2), (1, LANES))

    qf = q_ref[...].astype(F32)
    qn = jnp.sqrt(jnp.sum(qf * qf, axis=0, keepdims=True))
    for ks in range(n_k):
        m_ref[:, ks * wk:(ks + 1) * wk] = qn[:, ks * wk:(ks + 1) * wk] * kmax_ref[ks, :, 0:1]
    acc_ref[...] = jnp.zeros(acc_ref.shape, F32)
    l8_ref[...] = jnp.zeros(l8_ref.shape, F32)

    def fast_body(c, carry):
        p = jnp.exp2(scores(c) - m_ref[...])
        l8_ref[...] += jnp.sum(p.reshape(tk // 8, 8, w), axis=0)
        acc_ref[...] += _dot(v_ref[c], p.astype(BF16))
        return carry

    lax.fori_loop(0, n_chunks, fast_body, 0, unroll=True)
    l_ref[...] = jnp.sum(l8_ref[...], axis=0, keepdims=True)
    underflow = jnp.min(l_ref[...]) < L_MIN_SAFE

    @pl.when(underflow)
    def _():
        acc_ref[...] = jnp.zeros(acc_ref.shape, F32)
        m_ref[...] = jnp.full(m_ref.shape, NEG_BIG, F32)
        l_ref[...] = jnp.zeros(l_ref.shape, F32)

        def body(c, carry):
            s = scores(c)
            m_old = m_ref[...]
            m_new = jnp.maximum(m_old, jnp.max(s, axis=0, keepdims=True))
            alpha = jnp.exp2(m_old - m_new)
            p = jnp.exp2(s - m_new)
            l_ref[...] = alpha * l_ref[...] + jnp.sum(p, axis=0, keepdims=True)
            acc_ref[...] = alpha * acc_ref[...] + _dot(v_ref[c], p.astype(BF16))
            m_ref[...] = m_new
            return carry

        lax.fori_loop(0, n_chunks, body, 0)

    o = acc_ref[...] * (1.0 / l_ref[...])
    if not diff:
        for r in range(w // tq):
            o_ref[:, r * LANES:(r + 1) * LANES] = o[:, r * tq:(r + 1) * tq].T.astype(BF16)
    else:
        lam = (jnp.exp(jnp.sum(lq1_ref[...] * lk1_ref[...], axis=-1, keepdims=True))
               - jnp.exp(jnp.sum(lq2_ref[...] * lk2_ref[...], axis=-1, keepdims=True))
               + lambda_init)
        d = o[:, :tq] - lam * o[:, tq:]
        y = _rms(d.T, subln_ref[...]) * (1.0 - lambda_init)
        o_ref[...] = y.astype(BF16)


def _attn(qt, k, vt, *, tq, diff_args=None, lambda_init=0.0):
    b, g, nqt, hd, w = qt.shape
    n_k, nc, tk = k.shape[2], k.shape[3], k.shape[4]
    dv = vt.shape[3]
    diff = diff_args is not None
    width = dv if diff else (w // tq) * dv
    in_specs = [
        pl.BlockSpec((None, None, None, hd, w), lambda bi, gi, qi: (bi, gi, qi, 0, 0)),
        pl.BlockSpec((None, None, n_k, nc, tk, hd), lambda bi, gi, qi: (bi, gi, 0, 0, 0, 0),
                     pipeline_mode=pl.Buffered(1)),
        pl.BlockSpec((None, None, nc, dv, tk), lambda bi, gi, qi: (bi, gi, 0, 0, 0),
                     pipeline_mode=pl.Buffered(1)),
    ]
    args = [qt, k, vt]
    if diff:
        in_specs += [pl.BlockSpec((1, hd), lambda bi, gi, qi: (0, 0))] * 4
        in_specs += [pl.BlockSpec((1, dv), lambda bi, gi, qi: (0, 0))]
        args += list(diff_args)
    return pl.pallas_call(
        functools.partial(_attn_kernel, tq=tq, diff=diff, lambda_init=lambda_init),
        grid=(b, g, nqt),
        in_specs=in_specs,
        out_specs=pl.BlockSpec((None, tq, width), lambda bi, gi, qi: (bi, qi, gi)),
        out_shape=jax.ShapeDtypeStruct((b, nqt * tq, g * width), BF16),
        scratch_shapes=[
            pltpu.VMEM((dv, w), F32),
            pltpu.VMEM((1, w), F32),
            pltpu.VMEM((1, w), F32),
            pltpu.VMEM((8, w), F32),
            pltpu.VMEM((n_k, 1, LANES), F32),
        ],
        compiler_params=_params("parallel", "parallel", "arbitrary"),
        name="attn_diff" if diff else "attn_gqa",
    )(*args)


def _outproj_kernel(x_ref, a_ref, b_ref, wa_ref, wb_ref, o_ref):
    o_ref[...] = x_ref[...] + _dot(a_ref[...], wa_ref[...]) + _dot(b_ref[...], wb_ref[...])


def _outproj(x, a, b, w_out, *, tm):
    m, d = x.shape
    da, db = a.shape[1], b.shape[1]
    assert da == db and w_out.shape[0] == da + db
    return pl.pallas_call(
        _outproj_kernel,
        grid=(m // tm,),
        in_specs=[
            pl.BlockSpec((tm, d), lambda i: (i, 0)),
            pl.BlockSpec((tm, da), lambda i: (i, 0)),
            pl.BlockSpec((tm, db), lambda i: (i, 0)),
            pl.BlockSpec((da, d), lambda i: (0, 0)),
            pl.BlockSpec((db, d), lambda i: (1, 0)),
        ],
        out_specs=pl.BlockSpec((tm, d), lambda i: (i, 0)),
        out_shape=jax.ShapeDtypeStruct((m, d), F32),
        compiler_params=_params("parallel"),
        name="outproj",
    )(x, a, b, w_out, w_out)


def _rope_cs(pos, dim, theta):
    inv_freq = theta ** (-jnp.arange(0, dim, 2, dtype=F32) / dim)
    ang = pos[:, None] * inv_freq[None, :]
    return jnp.cos(ang), jnp.sin(ang)


AXIAL_ORDER = (2, 2)
PARTIAL_ORDER = (4, 2)


def _permute_heads(a, order):
    n1, n2 = order
    lead = a.shape[:-1]
    a5 = a.reshape(lead + (a.shape[-1] // HEAD_DIM, n1, n2, HEAD_DIM // (n1 * n2)))
    return jnp.swapaxes(a5, -3, -2).reshape(a.shape)


def _rope_tables(seq):
    rows = seq // GRID_W
    rc, rs = (jnp.repeat(a, GRID_W, axis=0) for a in _rope_cs(jnp.arange(rows, dtype=F32), AXIAL_DIM, A_ROPE_THETA))
    cc, cs = (jnp.tile(a, (rows, 1)) for a in _rope_cs(jnp.arange(GRID_W, dtype=F32), AXIAL_DIM, A_ROPE_THETA))
    pc, ps = _rope_cs(jnp.arange(seq, dtype=F32), PARTIAL_ROPE_DIM, PARTIAL_ROPE_THETA)
    cos_a = jnp.concatenate([rc, cc, rc, cc], axis=-1)
    sin_a = jnp.concatenate([-rs, -cs, rs, cs], axis=-1)
    rest = HEAD_DIM // 2 - PARTIAL_ROPE_DIM // 2
    one, zero = jnp.ones((seq, rest), F32), jnp.zeros((seq, rest), F32)
    cos_b = jnp.concatenate([pc, one, pc, one], axis=-1)
    sin_b = jnp.concatenate([-ps, zero, ps, zero], axis=-1)
    return (cos_a, sin_a), (cos_b, sin_b)


def _tiles(m, seq, d_ff):
    def fit(n, t):
        t = min(t, n)
        while n % t:
            t //= 2
        return t
    return dict(
        ffn1_tm=fit(m, 1024), ffn2_tm=fit(m, 1024), ffn_tf=fit(d_ff, 512),
        proj_tm=fit(seq, 512), out_tm=fit(m, 512),
        a_tq=fit(seq, 256), b_tq=fit(seq, 512), a_tk=fit(seq, 4096), b_tk=fit(seq, 4096),
    )


def kernel(x, ffn1_norm, ffn1_w_gu, ffn1_w_down, mix_norm, w_in, a_q_norm, a_k_norm, b_q_norm, b_k_norm,
           b_lambda_q1, b_lambda_k1, b_lambda_q2, b_lambda_k2, b_subln, w_out, ffn2_norm, ffn2_w_gu,
           ffn2_w_down, out_norm):
    bsz, seq, d = x.shape
    depth = w_in.shape[0]
    d_ff = ffn1_w_down.shape[1]
    m = bsz * seq
    hd = HEAD_DIM
    a_heads = d // (2 * hd)
    a_kv = a_heads // A_GROUP
    b_vdim = 2 * hd
    b_heads = d // (2 * b_vdim)
    a_q, a_kvw = a_heads * hd, a_kv * hd
    b_qk, b_v = b_heads * 2 * hd, b_heads * b_vdim
    t = _tiles(m, seq, d_ff)
    q_scale = (hd ** -0.5) * LOG2E
    tab_a, tab_b = _rope_tables(seq)

    x = x.reshape(m, d)
    for l in range(depth):
        lambda_init = 0.8 - 0.6 * math.exp(-0.3 * l)
        row = lambda v: v[l].reshape(1, -1).astype(F32)
        w_gu1, w_d1 = ffn1_w_gu[l].astype(BF16), ffn1_w_down[l].astype(BF16)
        w_gu2, w_d2 = ffn2_w_gu[l].astype(BF16), ffn2_w_down[l].astype(BF16)
        w_o = w_out[l].astype(BF16)
        c1 = a_q + a_kvw
        c2 = c1 + a_kvw
        c3 = c2 + 2 * b_qk
        w_i = w_in[l]
        w_qk_a, w_v_a = _permute_heads(w_i[:, :c1], AXIAL_ORDER).astype(BF16), w_i[:, c1:c2].astype(BF16)
        w_qk_b, w_v_b = _permute_heads(w_i[:, c2:c3], PARTIAL_ORDER).astype(BF16), w_i[:, c3:].astype(BF16)

        x, h = _ffn(x, row(ffn1_norm), w_gu1, w_d1, row(mix_norm),
                    emit_norm=True, final_norm=False, tm=t["ffn1_tm"], tf=t["ffn_tf"])

        tqa, tqb, tka, tkb, ptm = t["a_tq"], t["b_tq"], t["a_tk"], t["b_tk"], t["proj_tm"]
        gain_a = jnp.concatenate([jnp.tile(_permute_heads(row(a_q_norm), AXIAL_ORDER) * q_scale, (1, a_heads)),
                                  jnp.tile(_permute_heads(row(a_k_norm), AXIAL_ORDER), (1, a_kv))], axis=-1)
        qa_t, ka, va_t = _proj(h, w_qk_a, w_v_a, gain_a, *tab_a, bsz=bsz, seq=seq, n_qc=a_heads, n_kc=a_kv,
                               per_group=A_GROUP, dv=hd, tq=tqa, tk=tka, tm=ptm)
        gain_b = jnp.concatenate([jnp.tile(_permute_heads(row(b_q_norm), PARTIAL_ORDER) * q_scale, (1, 2 * b_heads)),
                                  jnp.tile(_permute_heads(row(b_k_norm), PARTIAL_ORDER), (1, 2 * b_heads))], axis=-1)
        qb_t, kb, vb_t = _proj(h, w_qk_b, w_v_b, gain_b, *tab_b, bsz=bsz, seq=seq, n_qc=2 * b_heads, n_kc=2 * b_heads,
                               per_group=2, dv=b_vdim, tq=tqb, tk=tkb, tm=ptm)
        ka = ka.reshape(bsz, a_kv, 1, seq // tka, tka, hd)
        kb = kb.reshape(bsz, b_heads, 2, seq // tkb, tkb, hd)

        out_a = _attn(qa_t, ka, va_t, tq=tqa)
        out_b = _attn(qb_t, kb, vb_t, tq=tqb,
                      diff_args=(row(b_lambda_q1), row(b_lambda_k1), row(b_lambda_q2), row(b_lambda_k2), row(b_subln)),
                      lambda_init=lambda_init)

        x = _outproj(x, out_a.reshape(m, a_q), out_b.reshape(m, b_v), w_o, tm=t["out_tm"])
        x, = _ffn(x, row(ffn2_norm), w_gu2, w_d2, row(out_norm),
                  emit_norm=False, final_norm=True, tm=t["ffn2_tm"], tf=t["ffn_tf"])
    return x.reshape(bsz, seq, d)
```

```python
import functools
import math

import jax
import jax.numpy as jnp
from jax import lax
from jax.experimental import pallas as pl
from jax.experimental.pallas import tpu as pltpu

HEAD_DIM = 128
GRID_W = 64
EPS = 1e-6
A_ROPE_THETA = 10000.0
PARTIAL_ROPE_THETA = 500000.0
AXIAL_DIM = HEAD_DIM // 2
PARTIAL_ROPE_DIM = HEAD_DIM // 4
A_GROUP = 4
LANES = 128
LOG2E = math.log2(math.e)
VMEM_LIMIT_BYTES = 63 * 1024 * 1024
NEG_BIG = -1e30
L_MIN_SAFE = 1e-20

BF16 = jnp.bfloat16
F32 = jnp.float32


def _dot(a, b):
    return jnp.dot(a, b, preferred_element_type=F32)


def _rms(x, gain):
    r = lax.rsqrt(jnp.mean(x * x, axis=-1, keepdims=True) + EPS)
    return (x * r) * gain


def _params(*semantics):
    return pltpu.CompilerParams(dimension_semantics=semantics, vmem_limit_bytes=VMEM_LIMIT_BYTES)


def _ffn_kernel(x_ref, gain_ref, wg_ref, wu_ref, wd_ref, ngain_ref, o_ref, *rest, emit_norm, final_norm, single_step):
    if emit_norm:
        hn_ref, xn_ref = rest
    else:
        (xn_ref,) = rest
    j = pl.program_id(1)
    last = pl.num_programs(1) - 1

    def step(first, final):
        if first:
            x = x_ref[...]
            xn = _rms(x, gain_ref[...]).astype(BF16)
            xn_ref[...] = xn
        else:
            xn = xn_ref[...]
        g = _dot(xn, wg_ref[...])
        u = _dot(xn, wu_ref[...])
        act = (0.5 * g / (1.0 + jnp.exp(-g))) * u
        o = (x_ref[...] if first else o_ref[...]) + _dot(act.astype(BF16), wd_ref[...])
        if final:
            y = _rms(o, ngain_ref[...])
            if emit_norm:
                hn_ref[...] = y.astype(BF16)
            if final_norm:
                o = y
        o_ref[...] = o

    if single_step:
        step(True, True)
    else:
        pl.when(j == 0)(lambda: step(True, False))
        pl.when(jnp.logical_and(j > 0, j < last))(lambda: step(False, False))
        pl.when(j == last)(lambda: step(False, True))


def _ffn(x, gain, w_gu, w_down, next_gain, *, emit_norm, final_norm, tm, tf):
    m, d = x.shape
    d_ff = w_down.shape[0]
    nf = d_ff // tf
    assert m % tm == 0 and d_ff % tf == 0
    out_shape = [jax.ShapeDtypeStruct((m, d), F32)]
    out_specs = [pl.BlockSpec((tm, d), lambda i, j: (i, 0))]
    if emit_norm:
        out_shape.append(jax.ShapeDtypeStruct((m, d), BF16))
        out_specs.append(pl.BlockSpec((tm, d), lambda i, j: (i, 0)))
    res = pl.pallas_call(
        functools.partial(_ffn_kernel, emit_norm=emit_norm, final_norm=final_norm, single_step=nf == 1),
        grid=(m // tm, nf),
        in_specs=[
            pl.BlockSpec((tm, d), lambda i, j: (i, 0)),
            pl.BlockSpec((1, d), lambda i, j: (0, 0)),
            pl.BlockSpec((d, tf), lambda i, j: (0, j)),
            pl.BlockSpec((d, tf), lambda i, j: (0, j + nf)),
            pl.BlockSpec((tf, d), lambda i, j: (j, 0)),
            pl.BlockSpec((1, d), lambda i, j: (0, 0)),
        ],
        out_specs=out_specs,
        out_shape=out_shape,
        scratch_shapes=[pltpu.VMEM((tm, d), BF16)],
        compiler_params=_params("parallel", "arbitrary"),
        name="ffn",
    )(x, gain, w_gu, w_gu, w_down, next_gain)
    return res


PROJ_COLS = 2 * LANES


def _proj_kernel(h_ref, wqk_ref, wv_ref, gain_ref, cos_ref, sin_ref, q_ref, k_ref, v_ref, acc_a, acc_b, *,
                 n_qc, n_kc, per_group, tq):
    i = pl.program_id(0)
    tm = h_ref.shape[0]
    dv = v_ref.shape[1]
    v0 = wqk_ref.shape[1]
    width = v0 + wv_ref.shape[1]

    @pl.when(i == 0)
    def _():
        acc_b[...] = jnp.zeros(acc_b.shape, F32)

    def step(mm_ref, ep_ref):
        for w_ref, base in ((wqk_ref, 0), (wv_ref, v0)):
            for c0 in range(0, w_ref.shape[1], PROJ_COLS):
                cols = min(PROJ_COLS, w_ref.shape[1] - c0)
                mm_ref[:, base + c0:base + c0 + cols] = _dot(h_ref[...], w_ref[:, c0:c0 + cols])
        cos, sin = cos_ref[...], sin_ref[...]
        for c in range(n_qc + n_kc):
            sl = slice(c * LANES, (c + 1) * LANES)
            y = _rms(ep_ref[:, sl], gain_ref[:, sl])
            y = y * cos + pltpu.roll(y, LANES // 2, 1) * sin
            if c < n_qc:
                g, r = divmod(c, per_group)
                yt = y.T.astype(BF16)
                for j in range(tm // tq):
                    q_ref[g, j, :, r * tq:(r + 1) * tq] = yt[:, j * tq:(j + 1) * tq]
            else:
                k_ref[c - n_qc] = y.astype(BF16)
        for g in range((width - v0) // dv):
            v_ref[g] = ep_ref[:, v0 + g * dv:v0 + (g + 1) * dv].T.astype(BF16)

    @pl.when(i % 2 == 0)
    def _():
        step(acc_a, acc_b)

    @pl.when(i % 2 == 1)
    def _():
        step(acc_b, acc_a)


def _proj(h, w_qk, w_v, gain, cos, sin, *, bsz, seq, n_qc, n_kc, per_group, dv, tq, tk, tm):
    m, d = h.shape
    qk_w, v_w = w_qk.shape[1], w_v.shape[1]
    width = qk_w + v_w
    groups, v_groups = n_qc // per_group, v_w // dv
    nt, per_chunk, n = seq // tm, tk // tm, m // tm
    assert tm % tq == 0 and tk % tm == 0 and seq % tk == 0 and v_w % dv == 0 and qk_w == (n_qc + n_kc) * LANES

    def cur(i):
        return jnp.minimum(i, n - 1)

    def prev(i):
        return jnp.maximum(i - 1, 0)

    return pl.pallas_call(
        functools.partial(_proj_kernel, n_qc=n_qc, n_kc=n_kc, per_group=per_group, tq=tq),
        grid=(n + 1,),
        in_specs=[
            pl.BlockSpec((tm, d), lambda i: (cur(i), 0)),
            pl.BlockSpec((d, qk_w), lambda i: (0, 0)),
            pl.BlockSpec((d, v_w), lambda i: (0, 0)),
            pl.BlockSpec((1, qk_w), lambda i: (0, 0)),
            pl.BlockSpec((tm, LANES), lambda i: (prev(i) % nt, 0)),
            pl.BlockSpec((tm, LANES), lambda i: (prev(i) % nt, 0)),
        ],
        out_specs=[
            pl.BlockSpec((None, groups, tm // tq, LANES, per_group * tq),
                         lambda i: (prev(i) // nt, 0, prev(i) % nt, 0, 0)),
            pl.BlockSpec((None, n_kc, tm, LANES), lambda i: (prev(i) // nt, 0, prev(i) % nt, 0)),
            pl.BlockSpec((None, v_groups, None, dv, tm),
                         lambda i: (prev(i) // nt, 0, (prev(i) % nt) // per_chunk, 0, (prev(i) % nt) % per_chunk)),
        ],
        out_shape=[
            jax.ShapeDtypeStruct((bsz, groups, seq // tq, LANES, per_group * tq), BF16),
            jax.ShapeDtypeStruct((bsz, n_kc, seq, LANES), BF16),
            jax.ShapeDtypeStruct((bsz, v_groups, seq // tk, dv, tk), BF16),
        ],
        scratch_shapes=[pltpu.VMEM((tm, width), F32), pltpu.VMEM((tm, width), F32)],
        compiler_params=_params("arbitrary"),
        name="proj",
    )(h, w_qk, w_v, gain, cos, sin)


def _attn_kernel(*refs, tq, diff, lambda_init):
    if diff:
        q_ref, k_ref, v_ref, lq1_ref, lk1_ref, lq2_ref, lk2_ref, subln_ref, o_ref, acc_ref, m_ref, l_ref, l8_ref, kmax_ref = refs
    else:
        q_ref, k_ref, v_ref, o_ref, acc_ref, m_ref, l_ref, l8_ref, kmax_ref = refs
    n_k, n_chunks, tk = k_ref.shape[0], k_ref.shape[1], k_ref.shape[2]
    w = q_ref.shape[1]
    wk = w // n_k

    def scores(c):
        parts = [_dot(k_ref[ks, c], q_ref[:, ks * wk:(ks + 1) * wk]) for ks in range(n_k)]
        return parts[0] if n_k == 1 else jnp.concatenate(parts, axis=1)

    @pl.when(pl.program_id(2) == 0)
    def _():
        for ks in range(n_k):
            def kbody(c, best):
                kf = k_ref[ks, c].astype(F32)
                return jnp.maximum(best, jnp.max(jnp.sum(kf * kf, axis=-1, keepdims=True), axis=0, keepdims=True))
            k2 = lax.fori_loop(0, n_chunks, kbody, jnp.zeros((1, 1), F32))
            kmax_ref[ks] = jnp.broadcast_to(jnp.sqrt(k2), (1, LANES))

    qf = q_ref[...].astype(F32)
    qn = jnp.sqrt(jnp.sum(qf * qf, axis=0, keepdims=True))
    for ks in range(n_k):
        m_ref[:, ks * wk:(ks + 1) * wk] = qn[:, ks * wk:(ks + 1) * wk] * kmax_ref[ks, :, 0:1]
    acc_ref[...] = jnp.zeros(acc_ref.shape, F32)
    l8_ref[...] = jnp.zeros(l8_ref.shape, F32)

    def fast_body(c, carry):
        p = jnp.exp2(scores(c) - m_ref[...])
        l8_ref[...] += jnp.sum(p.reshape(tk // 8, 8, w), axis=0)
        acc_ref[...] += _dot(v_ref[c], p.astype(BF16))
        return carry

    lax.fori_loop(0, n_chunks, fast_body, 0, unroll=True)
    l_ref[...] = jnp.sum(l8_ref[...], axis=0, keepdims=True)
    underflow = jnp.min(l_ref[...]) < L_MIN_SAFE

    @pl.when(underflow)
    def _():
        acc_ref[...] = jnp.zeros(acc_ref.shape, F32)
        m_ref[...] = jnp.full(m_ref.shape, NEG_BIG, F32)
        l_ref[...] = jnp.zeros(l_ref.shape, F32)

        def body(c, carry):
            s = scores(c)
            m_old = m_ref[...]
            m_new = jnp.maximum(m_old, jnp.max(s, axis=0, keepdims=True))
            alpha = jnp.exp2(m_old - m_new)
            p = jnp.exp2(s - m_new)
            l_ref[...] = alpha * l_ref[...] + jnp.sum(p, axis=0, keepdims=True)
            acc_ref[...] = alpha * acc_ref[...] + _dot(v_ref[c], p.astype(BF16))
            m_ref[...] = m_new
            return carry

        lax.fori_loop(0, n_chunks, body, 0)

    o = acc_ref[...] * (1.0 / l_ref[...])
    if not diff:
        for r in range(w // tq):
            o_ref[:, r * LANES:(r + 1) * LANES] = o[:, r * tq:(r + 1) * tq].T.astype(BF16)
    else:
        lam = (jnp.exp(jnp.sum(lq1_ref[...] * lk1_ref[...], axis=-1, keepdims=True))
               - jnp.exp(jnp.sum(lq2_ref[...] * lk2_ref[...], axis=-1, keepdims=True))
               + lambda_init)
        d = o[:, :tq] - lam * o[:, tq:]
        y = _rms(d.T, subln_ref[...]) * (1.0 - lambda_init)
        o_ref[...] = y.astype(BF16)


def _attn(qt, k, vt, *, tq, diff_args=None, lambda_init=0.0):
    b, g, nqt, hd, w = qt.shape
    n_k, nc, tk = k.shape[2], k.shape[3], k.shape[4]
    dv = vt.shape[3]
    diff = diff_args is not None
    width = dv if diff else (w // tq) * dv
    in_specs = [
        pl.BlockSpec((None, None, None, hd, w), lambda bi, gi, qi: (bi, gi, qi, 0, 0)),
        pl.BlockSpec((None, None, n_k, nc, tk, hd), lambda bi, gi, qi: (bi, gi, 0, 0, 0, 0),
                     pipeline_mode=pl.Buffered(1)),
        pl.BlockSpec((None, None, nc, dv, tk), lambda bi, gi, qi: (bi, gi, 0, 0, 0),
                     pipeline_mode=pl.Buffered(1)),
    ]
    args = [qt, k, vt]
    if diff:
        in_specs += [pl.BlockSpec((1, hd), lambda bi, gi, qi: (0, 0))] * 4
        in_specs += [pl.BlockSpec((1, dv), lambda bi, gi, qi: (0, 0))]
        args += list(diff_args)
    return pl.pallas_call(
        functools.partial(_attn_kernel, tq=tq, diff=diff, lambda_init=lambda_init),
        grid=(b, g, nqt),
        in_specs=in_specs,
        out_specs=pl.BlockSpec((None, tq, width), lambda bi, gi, qi: (bi, qi, gi)),
        out_shape=jax.ShapeDtypeStruct((b, nqt * tq, g * width), BF16),
        scratch_shapes=[
            pltpu.VMEM((dv, w), F32),
            pltpu.VMEM((1, w), F32),
            pltpu.VMEM((1, w), F32),
            pltpu.VMEM((8, w), F32),
            pltpu.VMEM((n_k, 1, LANES), F32),
        ],
        compiler_params=_params("parallel", "parallel", "arbitrary"),
        name="attn_diff" if diff else "attn_gqa",
    )(*args)


def _outproj_kernel(x_ref, a_ref, b_ref, wa_ref, wb_ref, o_ref):
    o_ref[...] = x_ref[...] + _dot(a_ref[...], wa_ref[...]) + _dot(b_ref[...], wb_ref[...])


def _outproj(x, a, b, w_out, *, tm):
    m, d = x.shape
    da, db = a.shape[1], b.shape[1]
    assert da == db and w_out.shape[0] == da + db
    return pl.pallas_call(
        _outproj_kernel,
        grid=(m // tm,),
        in_specs=[
            pl.BlockSpec((tm, d), lambda i: (i, 0)),
            pl.BlockSpec((tm, da), lambda i: (i, 0)),
            pl.BlockSpec((tm, db), lambda i: (i, 0)),
            pl.BlockSpec((da, d), lambda i: (0, 0)),
            pl.BlockSpec((db, d), lambda i: (1, 0)),
        ],
        out_specs=pl.BlockSpec((tm, d), lambda i: (i, 0)),
        out_shape=jax.ShapeDtypeStruct((m, d), F32),
        compiler_params=_params("parallel"),
        name="outproj",
    )(x, a, b, w_out, w_out)


def _rope_cs(pos, dim, theta):
    inv_freq = theta ** (-jnp.arange(0, dim, 2, dtype=F32) / dim)
    ang = pos[:, None] * inv_freq[None, :]
    return jnp.cos(ang), jnp.sin(ang)


AXIAL_ORDER = (2, 2)
PARTIAL_ORDER = (4, 2)


def _permute_heads(a, order):
    n1, n2 = order
    lead = a.shape[:-1]
    a5 = a.reshape(lead + (a.shape[-1] // HEAD_DIM, n1, n2, HEAD_DIM // (n1 * n2)))
    return jnp.swapaxes(a5, -3, -2).reshape(a.shape)


def _rope_tables(seq):
    rows = seq // GRID_W
    small = (_rope_cs(jnp.arange(rows, dtype=F32), AXIAL_DIM, A_ROPE_THETA)
             + _rope_cs(jnp.arange(GRID_W, dtype=F32), AXIAL_DIM, A_ROPE_THETA)
             + _rope_cs(jnp.arange(seq, dtype=F32), PARTIAL_ROPE_DIM, PARTIAL_ROPE_THETA))
    rc, rs, cc, cs, pc, ps = lax.optimization_barrier(small)
    rc, rs = jnp.repeat(rc, GRID_W, axis=0), jnp.repeat(rs, GRID_W, axis=0)
    cc, cs = jnp.tile(cc, (rows, 1)), jnp.tile(cs, (rows, 1))
    cos_a = jnp.concatenate([rc, cc, rc, cc], axis=-1)
    sin_a = jnp.concatenate([-rs, -cs, rs, cs], axis=-1)
    rest = HEAD_DIM // 2 - PARTIAL_ROPE_DIM // 2
    one, zero = jnp.ones((seq, rest), F32), jnp.zeros((seq, rest), F32)
    cos_b = jnp.concatenate([pc, one, pc, one], axis=-1)
    sin_b = jnp.concatenate([-ps, zero, ps, zero], axis=-1)
    return (cos_a, sin_a), (cos_b, sin_b)


def _tiles(m, seq, d_ff):
    def fit(n, t):
        t = min(t, n)
        while n % t:
            t //= 2
        return t
    return dict(
        ffn1_tm=fit(m, 1024), ffn2_tm=fit(m, 1024), ffn_tf=fit(d_ff, 512),
        proj_tm=fit(seq, 512), out_tm=fit(m, 512),
        a_tq=fit(seq, 256), b_tq=fit(seq, 512), a_tk=fit(seq, 4096), b_tk=fit(seq, 4096),
    )


def kernel(x, ffn1_norm, ffn1_w_gu, ffn1_w_down, mix_norm, w_in, a_q_norm, a_k_norm, b_q_norm, b_k_norm,
           b_lambda_q1, b_lambda_k1, b_lambda_q2, b_lambda_k2, b_subln, w_out, ffn2_norm, ffn2_w_gu,
           ffn2_w_down, out_norm):
    bsz, seq, d = x.shape
    depth = w_in.shape[0]
    d_ff = ffn1_w_down.shape[1]
    m = bsz * seq
    hd = HEAD_DIM
    a_heads = d // (2 * hd)
    a_kv = a_heads // A_GROUP
    b_vdim = 2 * hd
    b_heads = d // (2 * b_vdim)
    a_q, a_kvw = a_heads * hd, a_kv * hd
    b_qk, b_v = b_heads * 2 * hd, b_heads * b_vdim
    t = _tiles(m, seq, d_ff)
    q_scale = (hd ** -0.5) * LOG2E
    tab_a, tab_b = _rope_tables(seq)

    x = x.reshape(m, d)
    for l in range(depth):
        lambda_init = 0.8 - 0.6 * math.exp(-0.3 * l)
        row = lambda v: v[l].reshape(1, -1).astype(F32)
        w_gu1, w_d1 = ffn1_w_gu[l].astype(BF16), ffn1_w_down[l].astype(BF16)
        w_gu2, w_d2 = ffn2_w_gu[l].astype(BF16), ffn2_w_down[l].astype(BF16)
        w_o = w_out[l].astype(BF16)
        c1 = a_q + a_kvw
        c2 = c1 + a_kvw
        c3 = c2 + 2 * b_qk
        w_i = w_in[l]
        w_qk_a, w_v_a = _permute_heads(w_i[:, :c1], AXIAL_ORDER).astype(BF16), w_i[:, c1:c2].astype(BF16)
        w_qk_b, w_v_b = _permute_heads(w_i[:, c2:c3], PARTIAL_ORDER).astype(BF16), w_i[:, c3:].astype(BF16)

        x, h = _ffn(x, row(ffn1_norm), w_gu1, w_d1, row(mix_norm),
                    emit_norm=True, final_norm=False, tm=t["ffn1_tm"], tf=t["ffn_tf"])

        tqa, tqb, tka, tkb, ptm = t["a_tq"], t["b_tq"], t["a_tk"], t["b_tk"], t["proj_tm"]
        gain_a = jnp.concatenate([jnp.tile(_permute_heads(row(a_q_norm), AXIAL_ORDER) * q_scale, (1, a_heads)),
                                  jnp.tile(_permute_heads(row(a_k_norm), AXIAL_ORDER), (1, a_kv))], axis=-1)
        qa_t, ka, va_t = _proj(h, w_qk_a, w_v_a, gain_a, *tab_a, bsz=bsz, seq=seq, n_qc=a_heads, n_kc=a_kv,
                               per_group=A_GROUP, dv=hd, tq=tqa, tk=tka, tm=ptm)
        gain_b = jnp.concatenate([jnp.tile(_permute_heads(row(b_q_norm), PARTIAL_ORDER) * q_scale, (1, 2 * b_heads)),
                                  jnp.tile(_permute_heads(row(b_k_norm), PARTIAL_ORDER), (1, 2 * b_heads))], axis=-1)
        qb_t, kb, vb_t = _proj(h, w_qk_b, w_v_b, gain_b, *tab_b, bsz=bsz, seq=seq, n_qc=2 * b_heads, n_kc=2 * b_heads,
                               per_group=2, dv=b_vdim, tq=tqb, tk=tkb, tm=ptm)
        ka = ka.reshape(bsz, a_kv, 1, seq // tka, tka, hd)
        kb = kb.reshape(bsz, b_heads, 2, seq // tkb, tkb, hd)

        out_a = _attn(qa_t, ka, va_t, tq=tqa)
        out_b = _attn(qb_t, kb, vb_t, tq=tqb,
                      diff_args=(row(b_lambda_q1), row(b_lambda_k1), row(b_lambda_q2), row(b_lambda_k2), row(b_subln)),
                      lambda_init=lambda_init)

        x = _outproj(x, out_a.reshape(m, a_q), out_b.reshape(m, b_v), w_o, tm=t["out_tm"])
        x, = _ffn(x, row(ffn2_norm), w_gu2, w_d2, row(out_norm),
                  emit_norm=False, final_norm=True, tm=t["ffn2_tm"], tf=t["ffn_tf"])
    return x.reshape(bsz, seq, d)
```

```python
import functools
import math

import jax
import jax.numpy as jnp
from jax import lax
from jax.experimental import pallas as pl
from jax.experimental.pallas import tpu as pltpu

HEAD_DIM = 128
GRID_W = 64
EPS = 1e-6
A_ROPE_THETA = 10000.0
PARTIAL_ROPE_THETA = 500000.0
AXIAL_DIM = HEAD_DIM // 2
PARTIAL_ROPE_DIM = HEAD_DIM // 4
A_GROUP = 4
LANES = 128
LOG2E = math.log2(math.e)
VMEM_LIMIT_BYTES = 63 * 1024 * 1024
NEG_BIG = -1e30
L_MIN_SAFE = 1e-20

BF16 = jnp.bfloat16
F32 = jnp.float32


def _dot(a, b):
    return jnp.dot(a, b, preferred_element_type=F32)


def _rms(x, gain):
    r = lax.rsqrt(jnp.mean(x * x, axis=-1, keepdims=True) + EPS)
    return (x * r) * gain


def _params(*semantics):
    return pltpu.CompilerParams(dimension_semantics=semantics, vmem_limit_bytes=VMEM_LIMIT_BYTES)


def _ffn_up_kernel(x_ref, gain_ref, wg_ref, wu_ref, act_ref, xn_ref, *, single_step):
    def step(first):
        if first:
            xn = _rms(x_ref[...], gain_ref[...]).astype(BF16)
            xn_ref[...] = xn
        else:
            xn = xn_ref[...]
        g = _dot(xn, wg_ref[...])
        u = _dot(xn, wu_ref[...])
        act_ref[...] = ((0.5 * g / (1.0 + jnp.exp(-g))) * u).astype(BF16)

    if single_step:
        step(True)
    else:
        j = pl.program_id(1)
        pl.when(j == 0)(lambda: step(True))
        pl.when(j > 0)(lambda: step(False))


def _ffn_down_kernel(act_ref, wd_ref, x_ref, ngain_ref, o_ref, *rest, emit_norm, final_norm):
    o = x_ref[...] + _dot(act_ref[...], wd_ref[...])
    if emit_norm or final_norm:
        y = _rms(o, ngain_ref[...])
        if emit_norm:
            rest[0][...] = y.astype(BF16)
        if final_norm:
            o = y
    o_ref[...] = o


def _ffn(x, gain, w_gu, w_down, next_gain, *, emit_norm, final_norm, tm_up, tn, tm_down):
    m, d = x.shape
    d_ff = w_down.shape[0]
    nf = d_ff // tn
    assert m % tm_up == 0 and m % tm_down == 0 and d_ff % tn == 0
    act = pl.pallas_call(
        functools.partial(_ffn_up_kernel, single_step=nf == 1),
        grid=(m // tm_up, nf),
        in_specs=[
            pl.BlockSpec((tm_up, d), lambda i, j: (i, 0)),
            pl.BlockSpec((1, d), lambda i, j: (0, 0)),
            pl.BlockSpec((d, tn), lambda i, j: (0, j)),
            pl.BlockSpec((d, tn), lambda i, j: (0, j + nf)),
        ],
        out_specs=pl.BlockSpec((tm_up, tn), lambda i, j: (i, j)),
        out_shape=jax.ShapeDtypeStruct((m, d_ff), BF16),
        scratch_shapes=[pltpu.VMEM((tm_up, d), BF16)],
        compiler_params=_params("parallel", "arbitrary"),
        name="ffn_up",
    )(x, gain, w_gu, w_gu)

    out_shape = [jax.ShapeDtypeStruct((m, d), F32)]
    out_specs = [pl.BlockSpec((tm_down, d), lambda i: (i, 0))]
    if emit_norm:
        out_shape.append(jax.ShapeDtypeStruct((m, d), BF16))
        out_specs.append(pl.BlockSpec((tm_down, d), lambda i: (i, 0)))
    return pl.pallas_call(
        functools.partial(_ffn_down_kernel, emit_norm=emit_norm, final_norm=final_norm),
        grid=(m // tm_down,),
        in_specs=[
            pl.BlockSpec((tm_down, d_ff), lambda i: (i, 0)),
            pl.BlockSpec((d_ff, d), lambda i: (0, 0), pipeline_mode=pl.Buffered(1)),
            pl.BlockSpec((tm_down, d), lambda i: (i, 0)),
            pl.BlockSpec((1, d), lambda i: (0, 0)),
        ],
        out_specs=out_specs,
        out_shape=out_shape,
        compiler_params=_params("parallel"),
        name="ffn_down",
    )(act, w_down, x, next_gain)


PROJ_COLS = 2 * LANES


def _proj_kernel(h_ref, wqk_ref, wv_ref, gain_ref, cos_ref, sin_ref, q_ref, k_ref, v_ref, acc_a, acc_b, *,
                 n_qc, n_kc, per_group, tq):
    i = pl.program_id(0)
    tm = h_ref.shape[0]
    dv = v_ref.shape[1]
    v0 = wqk_ref.shape[1]
    width = v0 + wv_ref.shape[1]

    @pl.when(i == 0)
    def _():
        acc_b[...] = jnp.zeros(acc_b.shape, F32)

    def step(mm_ref, ep_ref):
        for w_ref, base in ((wqk_ref, 0), (wv_ref, v0)):
            for c0 in range(0, w_ref.shape[1], PROJ_COLS):
                cols = min(PROJ_COLS, w_ref.shape[1] - c0)
                mm_ref[:, base + c0:base + c0 + cols] = _dot(h_ref[...], w_ref[:, c0:c0 + cols])
        cos, sin = cos_ref[...], sin_ref[...]
        for c in range(n_qc + n_kc):
            sl = slice(c * LANES, (c + 1) * LANES)
            y = _rms(ep_ref[:, sl], gain_ref[:, sl])
            y = y * cos + pltpu.roll(y, LANES // 2, 1) * sin
            if c < n_qc:
                g, r = divmod(c, per_group)
                yt = y.T.astype(BF16)
                for j in range(tm // tq):
                    q_ref[g, j, :, r * tq:(r + 1) * tq] = yt[:, j * tq:(j + 1) * tq]
            else:
                k_ref[c - n_qc] = y.astype(BF16)
        for g in range((width - v0) // dv):
            v_ref[g] = ep_ref[:, v0 + g * dv:v0 + (g + 1) * dv].T.astype(BF16)

    @pl.when(i % 2 == 0)
    def _():
        step(acc_a, acc_b)

    @pl.when(i % 2 == 1)
    def _():
        step(acc_b, acc_a)


def _proj(h, w_qk, w_v, gain, cos, sin, *, bsz, seq, n_qc, n_kc, per_group, dv, tq, tk, tm):
    m, d = h.shape
    qk_w, v_w = w_qk.shape[1], w_v.shape[1]
    width = qk_w + v_w
    groups, v_groups = n_qc // per_group, v_w // dv
    nt, per_chunk, n = seq // tm, tk // tm, m // tm
    assert tm % tq == 0 and tk % tm == 0 and seq % tk == 0 and v_w % dv == 0 and qk_w == (n_qc + n_kc) * LANES

    def cur(i):
        return jnp.minimum(i, n - 1)

    def prev(i):
        return jnp.maximum(i - 1, 0)

    return pl.pallas_call(
        functools.partial(_proj_kernel, n_qc=n_qc, n_kc=n_kc, per_group=per_group, tq=tq),
        grid=(n + 1,),
        in_specs=[
            pl.BlockSpec((tm, d), lambda i: (cur(i), 0)),
            pl.BlockSpec((d, qk_w), lambda i: (0, 0)),
            pl.BlockSpec((d, v_w), lambda i: (0, 0)),
            pl.BlockSpec((1, qk_w), lambda i: (0, 0)),
            pl.BlockSpec((tm, LANES), lambda i: (prev(i) % nt, 0)),
            pl.BlockSpec((tm, LANES), lambda i: (prev(i) % nt, 0)),
        ],
        out_specs=[
            pl.BlockSpec((None, groups, tm // tq, LANES, per_group * tq),
                         lambda i: (prev(i) // nt, 0, prev(i) % nt, 0, 0)),
            pl.BlockSpec((None, n_kc, tm, LANES), lambda i: (prev(i) // nt, 0, prev(i) % nt, 0)),
            pl.BlockSpec((None, v_groups, None, dv, tm),
                         lambda i: (prev(i) // nt, 0, (prev(i) % nt) // per_chunk, 0, (prev(i) % nt) % per_chunk)),
        ],
        out_shape=[
            jax.ShapeDtypeStruct((bsz, groups, seq // tq, LANES, per_group * tq), BF16),
            jax.ShapeDtypeStruct((bsz, n_kc, seq, LANES), BF16),
            jax.ShapeDtypeStruct((bsz, v_groups, seq // tk, dv, tk), BF16),
        ],
        scratch_shapes=[pltpu.VMEM((tm, width), F32), pltpu.VMEM((tm, width), F32)],
        compiler_params=_params("arbitrary"),
        name="proj",
    )(h, w_qk, w_v, gain, cos, sin)


def _attn_kernel(*refs, tq, diff, lambda_init):
    if diff:
        q_ref, k_ref, v_ref, lq1_ref, lk1_ref, lq2_ref, lk2_ref, subln_ref, o_ref, acc_ref, m_ref, l_ref, l8_ref, kmax_ref = refs
    else:
        q_ref, k_ref, v_ref, o_ref, acc_ref, m_ref, l_ref, l8_ref, kmax_ref = refs
    n_k, n_chunks, tk = k_ref.shape[0], k_ref.shape[1], k_ref.shape[2]
    w = q_ref.shape[1]
    wk = w // n_k

    def scores(c):
        parts = [_dot(k_ref[ks, c], q_ref[:, ks * wk:(ks + 1) * wk]) for ks in range(n_k)]
        return parts[0] if n_k == 1 else jnp.concatenate(parts, axis=1)

    @pl.when(pl.program_id(2) == 0)
    def _():
        for ks in range(n_k):
            def kbody(c, best):
                kf = k_ref[ks, c].astype(F32)
                return jnp.maximum(best, jnp.max(jnp.sum(kf * kf, axis=-1, keepdims=True), axis=0, keepdims=True))
            k2 = lax.fori_loop(0, n_chunks, kbody, jnp.zeros((1, 1), F32))
            kmax_ref[ks] = jnp.broadcast_to(jnp.sqrt(k2), (1, LANES))

    qf = q_ref[...].astype(F32)
    qn = jnp.sqrt(jnp.sum(qf * qf, axis=0, keepdims=True))
    for ks in range(n_k):
        m_ref[:, ks * wk:(ks + 1) * wk] = qn[:, ks * wk:(ks + 1) * wk] * kmax_ref[ks, :, 0:1]
    acc_ref[...] = jnp.zeros(acc_ref.shape, F32)
    l8_ref[...] = jnp.zeros(l8_ref.shape, F32)

    def fast_body(c, carry):
        p = jnp.exp2(scores(c) - m_ref[...])
        l8_ref[...] += jnp.sum(p.reshape(tk // 8, 8, w), axis=0)
        acc_ref[...] += _dot(v_ref[c], p.astype(BF16))
        return carry

    lax.fori_loop(0, n_chunks, fast_body, 0, unroll=True)
    l_ref[...] = jnp.sum(l8_ref[...], axis=0, keepdims=True)
    underflow = jnp.min(l_ref[...]) < L_MIN_SAFE

    @pl.when(underflow)
    def _():
        acc_ref[...] = jnp.zeros(acc_ref.shape, F32)
        m_ref[...] = jnp.full(m_ref.shape, NEG_BIG, F32)
        l_ref[...] = jnp.zeros(l_ref.shape, F32)

        def body(c, carry):
            s = scores(c)
            m_old = m_ref[...]
            m_new = jnp.maximum(m_old, jnp.max(s, axis=0, keepdims=True))
            alpha = jnp.exp2(m_old - m_new)
            p = jnp.exp2(s - m_new)
            l_ref[...] = alpha * l_ref[...] + jnp.sum(p, axis=0, keepdims=True)
            acc_ref[...] = alpha * acc_ref[...] + _dot(v_ref[c], p.astype(BF16))
            m_ref[...] = m_new
            return carry

        lax.fori_loop(0, n_chunks, body, 0)

    o = acc_ref[...] * (1.0 / l_ref[...])
    if not diff:
        for r in range(w // tq):
            o_ref[:, r * LANES:(r + 1) * LANES] = o[:, r * tq:(r + 1) * tq].T.astype(BF16)
    else:
        lam = (jnp.exp(jnp.sum(lq1_ref[...] * lk1_ref[...], axis=-1, keepdims=True))
               - jnp.exp(jnp.sum(lq2_ref[...] * lk2_ref[...], axis=-1, keepdims=True))
               + lambda_init)
        d = o[:, :tq] - lam * o[:, tq:]
        y = _rms(d.T, subln_ref[...]) * (1.0 - lambda_init)
        o_ref[...] = y.astype(BF16)


def _attn(qt, k, vt, *, tq, diff_args=None, lambda_init=0.0):
    b, g, nqt, hd, w = qt.shape
    n_k, nc, tk = k.shape[2], k.shape[3], k.shape[4]
    dv = vt.shape[3]
    diff = diff_args is not None
    width = dv if diff else (w // tq) * dv
    in_specs = [
        pl.BlockSpec((None, None, None, hd, w), lambda bi, gi, qi: (bi, gi, qi, 0, 0)),
        pl.BlockSpec((None, None, n_k, nc, tk, hd), lambda bi, gi, qi: (bi, gi, 0, 0, 0, 0),
                     pipeline_mode=pl.Buffered(1)),
        pl.BlockSpec((None, None, nc, dv, tk), lambda bi, gi, qi: (bi, gi, 0, 0, 0),
                     pipeline_mode=pl.Buffered(1)),
    ]
    args = [qt, k, vt]
    if diff:
        in_specs += [pl.BlockSpec((1, hd), lambda bi, gi, qi: (0, 0))] * 4
        in_specs += [pl.BlockSpec((1, dv), lambda bi, gi, qi: (0, 0))]
        args += list(diff_args)
    return pl.pallas_call(
        functools.partial(_attn_kernel, tq=tq, diff=diff, lambda_init=lambda_init),
        grid=(b, g, nqt),
        in_specs=in_specs,
        out_specs=pl.BlockSpec((None, tq, width), lambda bi, gi, qi: (bi, qi, gi)),
        out_shape=jax.ShapeDtypeStruct((b, nqt * tq, g * width), BF16),
        scratch_shapes=[
            pltpu.VMEM((dv, w), F32),
            pltpu.VMEM((1, w), F32),
            pltpu.VMEM((1, w), F32),
            pltpu.VMEM((8, w), F32),
            pltpu.VMEM((n_k, 1, LANES), F32),
        ],
        compiler_params=_params("parallel", "parallel", "arbitrary"),
        name="attn_diff" if diff else "attn_gqa",
    )(*args)


def _outproj_kernel(x_ref, a_ref, b_ref, wa_ref, wb_ref, o_ref):
    o_ref[...] = x_ref[...] + _dot(a_ref[...], wa_ref[...]) + _dot(b_ref[...], wb_ref[...])


def _outproj(x, a, b, w_out, *, tm):
    m, d = x.shape
    da, db = a.shape[1], b.shape[1]
    assert da == db and w_out.shape[0] == da + db
    return pl.pallas_call(
        _outproj_kernel,
        grid=(m // tm,),
        in_specs=[
            pl.BlockSpec((tm, d), lambda i: (i, 0)),
            pl.BlockSpec((tm, da), lambda i: (i, 0)),
            pl.BlockSpec((tm, db), lambda i: (i, 0)),
            pl.BlockSpec((da, d), lambda i: (0, 0)),
            pl.BlockSpec((db, d), lambda i: (1, 0)),
        ],
        out_specs=pl.BlockSpec((tm, d), lambda i: (i, 0)),
        out_shape=jax.ShapeDtypeStruct((m, d), F32),
        compiler_params=_params("parallel"),
        name="outproj",
    )(x, a, b, w_out, w_out)


def _rope_cs(pos, dim, theta):
    inv_freq = theta ** (-jnp.arange(0, dim, 2, dtype=F32) / dim)
    ang = pos[:, None] * inv_freq[None, :]
    return jnp.cos(ang), jnp.sin(ang)


AXIAL_ORDER = (2, 2)
PARTIAL_ORDER = (4, 2)


def _permute_heads(a, order):
    n1, n2 = order
    lead = a.shape[:-1]
    a5 = a.reshape(lead + (a.shape[-1] // HEAD_DIM, n1, n2, HEAD_DIM // (n1 * n2)))
    return jnp.swapaxes(a5, -3, -2).reshape(a.shape)


def _rope_tables(seq):
    rows = seq // GRID_W
    rc, rs = (jnp.repeat(a, GRID_W, axis=0) for a in _rope_cs(jnp.arange(rows, dtype=F32), AXIAL_DIM, A_ROPE_THETA))
    cc, cs = (jnp.tile(a, (rows, 1)) for a in _rope_cs(jnp.arange(GRID_W, dtype=F32), AXIAL_DIM, A_ROPE_THETA))
    pc, ps = _rope_cs(jnp.arange(seq, dtype=F32), PARTIAL_ROPE_DIM, PARTIAL_ROPE_THETA)
    cos_a = jnp.concatenate([rc, cc, rc, cc], axis=-1)
    sin_a = jnp.concatenate([-rs, -cs, rs, cs], axis=-1)
    rest = HEAD_DIM // 2 - PARTIAL_ROPE_DIM // 2
    one, zero = jnp.ones((seq, rest), F32), jnp.zeros((seq, rest), F32)
    cos_b = jnp.concatenate([pc, one, pc, one], axis=-1)
    sin_b = jnp.concatenate([-ps, zero, ps, zero], axis=-1)
    return (cos_a, sin_a), (cos_b, sin_b)


def _tiles(m, seq, d_ff):
    def fit(n, t):
        t = min(t, n)
        while n % t:
            t //= 2
        return t
    return dict(
        ffn_tm_up=fit(m, 1024), ffn_tn=fit(d_ff, 512), ffn_tm_down=fit(m, 512),
        proj_tm=fit(seq, 512), out_tm=fit(m, 512),
        a_tq=fit(seq, 256), b_tq=fit(seq, 512), a_tk=fit(seq, 4096), b_tk=fit(seq, 4096),
    )


def kernel(x, ffn1_norm, ffn1_w_gu, ffn1_w_down, mix_norm, w_in, a_q_norm, a_k_norm, b_q_norm, b_k_norm,
           b_lambda_q1, b_lambda_k1, b_lambda_q2, b_lambda_k2, b_subln, w_out, ffn2_norm, ffn2_w_gu,
           ffn2_w_down, out_norm):
    bsz, seq, d = x.shape
    depth = w_in.shape[0]
    d_ff = ffn1_w_down.shape[1]
    m = bsz * seq
    hd = HEAD_DIM
    a_heads = d // (2 * hd)
    a_kv = a_heads // A_GROUP
    b_vdim = 2 * hd
    b_heads = d // (2 * b_vdim)
    a_q, a_kvw = a_heads * hd, a_kv * hd
    b_qk, b_v = b_heads * 2 * hd, b_heads * b_vdim
    t = _tiles(m, seq, d_ff)
    q_scale = (hd ** -0.5) * LOG2E
    tab_a, tab_b = _rope_tables(seq)

    x = x.reshape(m, d)
    for l in range(depth):
        lambda_init = 0.8 - 0.6 * math.exp(-0.3 * l)
        row = lambda v: v[l].reshape(1, -1).astype(F32)
        w_gu1, w_d1 = ffn1_w_gu[l].astype(BF16), ffn1_w_down[l].astype(BF16)
        w_gu2, w_d2 = ffn2_w_gu[l].astype(BF16), ffn2_w_down[l].astype(BF16)
        w_o = w_out[l].astype(BF16)
        c1 = a_q + a_kvw
        c2 = c1 + a_kvw
        c3 = c2 + 2 * b_qk
        w_i = w_in[l]
        w_qk_a, w_v_a = _permute_heads(w_i[:, :c1], AXIAL_ORDER).astype(BF16), w_i[:, c1:c2].astype(BF16)
        w_qk_b, w_v_b = _permute_heads(w_i[:, c2:c3], PARTIAL_ORDER).astype(BF16), w_i[:, c3:].astype(BF16)

        x, h = _ffn(x, row(ffn1_norm), w_gu1, w_d1, row(mix_norm),
                    emit_norm=True, final_norm=False, tm_up=t["ffn_tm_up"], tn=t["ffn_tn"], tm_down=t["ffn_tm_down"])

        tqa, tqb, tka, tkb, ptm = t["a_tq"], t["b_tq"], t["a_tk"], t["b_tk"], t["proj_tm"]
        gain_a = jnp.concatenate([jnp.tile(_permute_heads(row(a_q_norm), AXIAL_ORDER) * q_scale, (1, a_heads)),
                                  jnp.tile(_permute_heads(row(a_k_norm), AXIAL_ORDER), (1, a_kv))], axis=-1)
        qa_t, ka, va_t = _proj(h, w_qk_a, w_v_a, gain_a, *tab_a, bsz=bsz, seq=seq, n_qc=a_heads, n_kc=a_kv,
                               per_group=A_GROUP, dv=hd, tq=tqa, tk=tka, tm=ptm)
        gain_b = jnp.concatenate([jnp.tile(_permute_heads(row(b_q_norm), PARTIAL_ORDER) * q_scale, (1, 2 * b_heads)),
                                  jnp.tile(_permute_heads(row(b_k_norm), PARTIAL_ORDER), (1, 2 * b_heads))], axis=-1)
        qb_t, kb, vb_t = _proj(h, w_qk_b, w_v_b, gain_b, *tab_b, bsz=bsz, seq=seq, n_qc=2 * b_heads, n_kc=2 * b_heads,
                               per_group=2, dv=b_vdim, tq=tqb, tk=tkb, tm=ptm)
        ka = ka.reshape(bsz, a_kv, 1, seq // tka, tka, hd)
        kb = kb.reshape(bsz, b_heads, 2, seq // tkb, tkb, hd)

        out_a = _attn(qa_t, ka, va_t, tq=tqa)
        out_b = _attn(qb_t, kb, vb_t, tq=tqb,
                      diff_args=(row(b_lambda_q1), row(b_lambda_k1), row(b_lambda_q2), row(b_lambda_k2), row(b_subln)),
                      lambda_init=lambda_init)

        x = _outproj(x, out_a.reshape(m, a_q), out_b.reshape(m, b_v), w_o, tm=t["out_tm"])
        x, = _ffn(x, row(ffn2_norm), w_gu2, w_d2, row(out_norm),
                  emit_norm=False, final_norm=True, tm_up=t["ffn_tm_up"], tn=t["ffn_tn"], tm_down=t["ffn_tm_down"])
    return x.reshape(bsz, seq, d)
```

```python
import functools
import math

import jax
import jax.numpy as jnp
from jax import lax
from jax.experimental import pallas as pl
from jax.experimental.pallas import tpu as pltpu

HEAD_DIM = 128
GRID_W = 64
EPS = 1e-6
A_ROPE_THETA = 10000.0
PARTIAL_ROPE_THETA = 500000.0
AXIAL_DIM = HEAD_DIM // 2
PARTIAL_ROPE_DIM = HEAD_DIM // 4
A_GROUP = 4
LANES = 128
LOG2E = math.log2(math.e)
VMEM_LIMIT_BYTES = 63 * 1024 * 1024
NEG_BIG = -1e30
L_MIN_SAFE = 1e-20

BF16 = jnp.bfloat16
F32 = jnp.float32


def _dot(a, b):
    return jnp.dot(a, b, preferred_element_type=F32)


def _rms(x, gain):
    r = lax.rsqrt(jnp.mean(x * x, axis=-1, keepdims=True) + EPS)
    return (x * r) * gain


def _params(*semantics):
    return pltpu.CompilerParams(dimension_semantics=semantics, vmem_limit_bytes=VMEM_LIMIT_BYTES)


def _ffn_kernel(x_ref, gain_ref, wg_ref, wu_ref, wd_ref, ngain_ref, o_ref, *rest, emit_norm, final_norm, single_step):
    if emit_norm:
        hn_ref, xn_ref = rest
    else:
        (xn_ref,) = rest
    j = pl.program_id(1)
    last = pl.num_programs(1) - 1

    def step(first, final):
        if first:
            x = x_ref[...]
            xn = _rms(x, gain_ref[...]).astype(BF16)
            xn_ref[...] = xn
        else:
            xn = xn_ref[...]
        g = _dot(xn, wg_ref[...])
        u = _dot(xn, wu_ref[...])
        act = (0.5 * g / (1.0 + jnp.exp(-g))) * u
        o = (x_ref[...] if first else o_ref[...]) + _dot(act.astype(BF16), wd_ref[...])
        if final:
            y = _rms(o, ngain_ref[...])
            if emit_norm:
                hn_ref[...] = y.astype(BF16)
            if final_norm:
                o = y
        o_ref[...] = o

    if single_step:
        step(True, True)
    else:
        pl.when(j == 0)(lambda: step(True, False))
        pl.when(jnp.logical_and(j > 0, j < last))(lambda: step(False, False))
        pl.when(j == last)(lambda: step(False, True))


def _ffn(x, gain, w_gu, w_down, next_gain, *, emit_norm, final_norm, tm, tf):
    m, d = x.shape
    d_ff = w_down.shape[0]
    nf = d_ff // tf
    assert m % tm == 0 and d_ff % tf == 0
    out_shape = [jax.ShapeDtypeStruct((m, d), F32)]
    out_specs = [pl.BlockSpec((tm, d), lambda i, j: (i, 0))]
    if emit_norm:
        out_shape.append(jax.ShapeDtypeStruct((m, d), BF16))
        out_specs.append(pl.BlockSpec((tm, d), lambda i, j: (i, 0)))
    res = pl.pallas_call(
        functools.partial(_ffn_kernel, emit_norm=emit_norm, final_norm=final_norm, single_step=nf == 1),
        grid=(m // tm, nf),
        in_specs=[
            pl.BlockSpec((tm, d), lambda i, j: (i, 0)),
            pl.BlockSpec((1, d), lambda i, j: (0, 0)),
            pl.BlockSpec((d, tf), lambda i, j: (0, j)),
            pl.BlockSpec((d, tf), lambda i, j: (0, j + nf)),
            pl.BlockSpec((tf, d), lambda i, j: (j, 0)),
            pl.BlockSpec((1, d), lambda i, j: (0, 0)),
        ],
        out_specs=out_specs,
        out_shape=out_shape,
        scratch_shapes=[pltpu.VMEM((tm, d), BF16)],
        compiler_params=_params("parallel", "arbitrary"),
        name="ffn",
    )(x, gain, w_gu, w_gu, w_down, next_gain)
    return res


PROJ_COLS = 2 * LANES


def _proj_kernel(h_ref, wqk_ref, wv_ref, gain_ref, cos_ref, sin_ref, q_ref, k_ref, v_ref, acc_a, acc_b, *,
                 n_qc, n_kc, per_group, tq):
    i = pl.program_id(0)
    tm = h_ref.shape[0]
    dv = v_ref.shape[1]
    v0 = wqk_ref.shape[1]
    width = v0 + wv_ref.shape[1]

    @pl.when(i == 0)
    def _():
        acc_b[...] = jnp.zeros(acc_b.shape, F32)

    def step(mm_ref, ep_ref):
        for w_ref, base in ((wqk_ref, 0), (wv_ref, v0)):
            for c0 in range(0, w_ref.shape[1], PROJ_COLS):
                cols = min(PROJ_COLS, w_ref.shape[1] - c0)
                mm_ref[:, base + c0:base + c0 + cols] = _dot(h_ref[...], w_ref[:, c0:c0 + cols])
        cos, sin = cos_ref[...], sin_ref[...]
        for c in range(n_qc + n_kc):
            sl = slice(c * LANES, (c + 1) * LANES)
            y = _rms(ep_ref[:, sl], gain_ref[:, sl])
            y = y * cos + pltpu.roll(y, LANES // 2, 1) * sin
            if c < n_qc:
                g, r = divmod(c, per_group)
                yt = y.T.astype(BF16)
                for j in range(tm // tq):
                    q_ref[g, j, :, r * tq:(r + 1) * tq] = yt[:, j * tq:(j + 1) * tq]
            else:
                k_ref[c - n_qc] = y.astype(BF16)
        for g in range((width - v0) // dv):
            v_ref[g] = ep_ref[:, v0 + g * dv:v0 + (g + 1) * dv].T.astype(BF16)

    @pl.when(i % 2 == 0)
    def _():
        step(acc_a, acc_b)

    @pl.when(i % 2 == 1)
    def _():
        step(acc_b, acc_a)


def _proj(h, w_qk, w_v, gain, cos, sin, *, bsz, seq, n_qc, n_kc, per_group, dv, tq, tk, tm):
    m, d = h.shape
    qk_w, v_w = w_qk.shape[1], w_v.shape[1]
    width = qk_w + v_w
    groups, v_groups = n_qc // per_group, v_w // dv
    nt, per_chunk, n = seq // tm, tk // tm, m // tm
    assert tm % tq == 0 and tk % tm == 0 and seq % tk == 0 and v_w % dv == 0 and qk_w == (n_qc + n_kc) * LANES

    def cur(i):
        return jnp.minimum(i, n - 1)

    def prev(i):
        return jnp.maximum(i - 1, 0)

    return pl.pallas_call(
        functools.partial(_proj_kernel, n_qc=n_qc, n_kc=n_kc, per_group=per_group, tq=tq),
        grid=(n + 1,),
        in_specs=[
            pl.BlockSpec((tm, d), lambda i: (cur(i), 0)),
            pl.BlockSpec((d, qk_w), lambda i: (0, 0)),
            pl.BlockSpec((d, v_w), lambda i: (0, 0)),
            pl.BlockSpec((1, qk_w), lambda i: (0, 0)),
            pl.BlockSpec((tm, LANES), lambda i: (prev(i) % nt, 0)),
            pl.BlockSpec((tm, LANES), lambda i: (prev(i) % nt, 0)),
        ],
        out_specs=[
            pl.BlockSpec((None, groups, tm // tq, LANES, per_group * tq),
                         lambda i: (prev(i) // nt, 0, prev(i) % nt, 0, 0)),
            pl.BlockSpec((None, n_kc, tm, LANES), lambda i: (prev(i) // nt, 0, prev(i) % nt, 0)),
            pl.BlockSpec((None, v_groups, None, dv, tm),
                         lambda i: (prev(i) // nt, 0, (prev(i) % nt) // per_chunk, 0, (prev(i) % nt) % per_chunk)),
        ],
        out_shape=[
            jax.ShapeDtypeStruct((bsz, groups, seq // tq, LANES, per_group * tq), BF16),
            jax.ShapeDtypeStruct((bsz, n_kc, seq, LANES), BF16),
            jax.ShapeDtypeStruct((bsz, v_groups, seq // tk, dv, tk), BF16),
        ],
        scratch_shapes=[pltpu.VMEM((tm, width), F32), pltpu.VMEM((tm, width), F32)],
        compiler_params=_params("arbitrary"),
        name="proj",
    )(h, w_qk, w_v, gain, cos, sin)


def _attn_kernel(*refs, tq, diff, lambda_init):
    if diff:
        q_ref, k_ref, v_ref, lq1_ref, lk1_ref, lq2_ref, lk2_ref, subln_ref, o_ref, acc_ref, m_ref, l_ref, l8_ref, kmax_ref = refs
    else:
        q_ref, k_ref, v_ref, o_ref, acc_ref, m_ref, l_ref, l8_ref, kmax_ref = refs
    n_k, n_chunks, tk = k_ref.shape[0], k_ref.shape[1], k_ref.shape[2]
    n_tiles, w = q_ref.shape[0], q_ref.shape[2]
    wk = w // n_k

    def scores(t, c):
        parts = [_dot(k_ref[ks, c], q_ref[t, :, ks * wk:(ks + 1) * wk]) for ks in range(n_k)]
        return parts[0] if n_k == 1 else jnp.concatenate(parts, axis=1)

    @pl.when(pl.program_id(2) == 0)
    def _():
        for ks in range(n_k):
            def kbody(c, best):
                kf = k_ref[ks, c].astype(F32)
                return jnp.maximum(best, jnp.max(jnp.sum(kf * kf, axis=-1, keepdims=True), axis=0, keepdims=True))
            k2 = lax.fori_loop(0, n_chunks, kbody, jnp.zeros((1, 1), F32))
            kmax_ref[ks] = jnp.broadcast_to(jnp.sqrt(k2), (1, LANES))

    def fast_pass(t):
        qf = q_ref[t].astype(F32)
        qn = jnp.sqrt(jnp.sum(qf * qf, axis=0, keepdims=True))
        for ks in range(n_k):
            m_ref[t, :, ks * wk:(ks + 1) * wk] = qn[:, ks * wk:(ks + 1) * wk] * kmax_ref[ks, :, 0:1]
        acc_ref[t] = jnp.zeros(acc_ref.shape[1:], F32)
        l8_ref[t] = jnp.zeros(l8_ref.shape[1:], F32)
        for c in range(n_chunks):
            p = jnp.exp2(scores(t, c) - m_ref[t])
            l8_ref[t] += jnp.sum(p.reshape(tk // 8, 8, w), axis=0)
            acc_ref[t] += _dot(v_ref[c], p.astype(BF16))
        l_ref[t] = jnp.sum(l8_ref[t], axis=0, keepdims=True)

    def exact_pass_if_underflow(t):
        @pl.when(jnp.min(l_ref[t]) < L_MIN_SAFE)
        def _():
            acc_ref[t] = jnp.zeros(acc_ref.shape[1:], F32)
            m_ref[t] = jnp.full(m_ref.shape[1:], NEG_BIG, F32)
            l_ref[t] = jnp.zeros(l_ref.shape[1:], F32)

            def body(c, carry):
                s = scores(t, c)
                m_old = m_ref[t]
                m_new = jnp.maximum(m_old, jnp.max(s, axis=0, keepdims=True))
                alpha = jnp.exp2(m_old - m_new)
                p = jnp.exp2(s - m_new)
                l_ref[t] = alpha * l_ref[t] + jnp.sum(p, axis=0, keepdims=True)
                acc_ref[t] = alpha * acc_ref[t] + _dot(v_ref[c], p.astype(BF16))
                m_ref[t] = m_new
                return carry

            lax.fori_loop(0, n_chunks, body, 0)

    def epilogue(t):
        rows = slice(t * tq, (t + 1) * tq)
        o = acc_ref[t] * (1.0 / l_ref[t])
        if not diff:
            for r in range(w // tq):
                o_ref[rows, r * LANES:(r + 1) * LANES] = o[:, r * tq:(r + 1) * tq].T.astype(BF16)
        else:
            lam = (jnp.exp(jnp.sum(lq1_ref[...] * lk1_ref[...], axis=-1, keepdims=True))
                   - jnp.exp(jnp.sum(lq2_ref[...] * lk2_ref[...], axis=-1, keepdims=True))
                   + lambda_init)
            d = o[:, :tq] - lam * o[:, tq:]
            y = _rms(d.T, subln_ref[...]) * (1.0 - lambda_init)
            o_ref[rows, :] = y.astype(BF16)

    for t in range(n_tiles):
        fast_pass(t)
        if t > 0:
            epilogue(t - 1)
        exact_pass_if_underflow(t)
    epilogue(n_tiles - 1)


def _attn(qt, k, vt, *, tq, tiles_per_step, diff_args=None, lambda_init=0.0):
    b, g, nqt, hd, w = qt.shape
    n_k, nc, tk = k.shape[2], k.shape[3], k.shape[4]
    dv = vt.shape[3]
    diff = diff_args is not None
    width = dv if diff else (w // tq) * dv
    nt = tiles_per_step
    assert nqt % nt == 0
    in_specs = [
        pl.BlockSpec((None, None, nt, hd, w), lambda bi, gi, qi: (bi, gi, qi, 0, 0)),
        pl.BlockSpec((None, None, n_k, nc, tk, hd), lambda bi, gi, qi: (bi, gi, 0, 0, 0, 0),
                     pipeline_mode=pl.Buffered(1)),
        pl.BlockSpec((None, None, nc, dv, tk), lambda bi, gi, qi: (bi, gi, 0, 0, 0),
                     pipeline_mode=pl.Buffered(1)),
    ]
    args = [qt, k, vt]
    if diff:
        in_specs += [pl.BlockSpec((1, hd), lambda bi, gi, qi: (0, 0))] * 4
        in_specs += [pl.BlockSpec((1, dv), lambda bi, gi, qi: (0, 0))]
        args += list(diff_args)
    return pl.pallas_call(
        functools.partial(_attn_kernel, tq=tq, diff=diff, lambda_init=lambda_init),
        grid=(b, g, nqt // nt),
        in_specs=in_specs,
        out_specs=pl.BlockSpec((None, nt * tq, width), lambda bi, gi, qi: (bi, qi, gi)),
        out_shape=jax.ShapeDtypeStruct((b, nqt * tq, g * width), BF16),
        scratch_shapes=[
            pltpu.VMEM((nt, dv, w), F32),
            pltpu.VMEM((nt, 1, w), F32),
            pltpu.VMEM((nt, 1, w), F32),
            pltpu.VMEM((nt, 8, w), F32),
            pltpu.VMEM((n_k, 1, LANES), F32),
        ],
        compiler_params=_params("parallel", "parallel", "arbitrary"),
        name="attn_diff" if diff else "attn_gqa",
    )(*args)


def _outproj_kernel(x_ref, a_ref, b_ref, wa_ref, wb_ref, o_ref):
    o_ref[...] = x_ref[...] + _dot(a_ref[...], wa_ref[...]) + _dot(b_ref[...], wb_ref[...])


def _outproj(x, a, b, w_out, *, tm):
    m, d = x.shape
    da, db = a.shape[1], b.shape[1]
    assert da == db and w_out.shape[0] == da + db
    return pl.pallas_call(
        _outproj_kernel,
        grid=(m // tm,),
        in_specs=[
            pl.BlockSpec((tm, d), lambda i: (i, 0)),
            pl.BlockSpec((tm, da), lambda i: (i, 0)),
            pl.BlockSpec((tm, db), lambda i: (i, 0)),
            pl.BlockSpec((da, d), lambda i: (0, 0)),
            pl.BlockSpec((db, d), lambda i: (1, 0)),
        ],
        out_specs=pl.BlockSpec((tm, d), lambda i: (i, 0)),
        out_shape=jax.ShapeDtypeStruct((m, d), F32),
        compiler_params=_params("parallel"),
        name="outproj",
    )(x, a, b, w_out, w_out)


def _rope_cs(pos, dim, theta):
    inv_freq = theta ** (-jnp.arange(0, dim, 2, dtype=F32) / dim)
    ang = pos[:, None] * inv_freq[None, :]
    return jnp.cos(ang), jnp.sin(ang)


AXIAL_ORDER = (2, 2)
PARTIAL_ORDER = (4, 2)


def _permute_heads(a, order):
    n1, n2 = order
    lead = a.shape[:-1]
    a5 = a.reshape(lead + (a.shape[-1] // HEAD_DIM, n1, n2, HEAD_DIM // (n1 * n2)))
    return jnp.swapaxes(a5, -3, -2).reshape(a.shape)


def _rope_tables(seq):
    rows = seq // GRID_W
    rc, rs = (jnp.repeat(a, GRID_W, axis=0) for a in _rope_cs(jnp.arange(rows, dtype=F32), AXIAL_DIM, A_ROPE_THETA))
    cc, cs = (jnp.tile(a, (rows, 1)) for a in _rope_cs(jnp.arange(GRID_W, dtype=F32), AXIAL_DIM, A_ROPE_THETA))
    pc, ps = _rope_cs(jnp.arange(seq, dtype=F32), PARTIAL_ROPE_DIM, PARTIAL_ROPE_THETA)
    cos_a = jnp.concatenate([rc, cc, rc, cc], axis=-1)
    sin_a = jnp.concatenate([-rs, -cs, rs, cs], axis=-1)
    rest = HEAD_DIM // 2 - PARTIAL_ROPE_DIM // 2
    one, zero = jnp.ones((seq, rest), F32), jnp.zeros((seq, rest), F32)
    cos_b = jnp.concatenate([pc, one, pc, one], axis=-1)
    sin_b = jnp.concatenate([-ps, zero, ps, zero], axis=-1)
    return (cos_a, sin_a), (cos_b, sin_b)


def _tiles(m, seq, d_ff):
    def fit(n, t):
        t = min(t, n)
        while n % t:
            t //= 2
        return t
    return dict(
        ffn1_tm=fit(m, 1024), ffn2_tm=fit(m, 1024), ffn_tf=fit(d_ff, 512),
        proj_tm=fit(seq, 512), out_tm=fit(m, 512),
        a_tq=fit(seq, 256), b_tq=fit(seq, 512), a_tk=fit(seq, 4096), b_tk=fit(seq, 4096), attn_tiles=2,
    )


def kernel(x, ffn1_norm, ffn1_w_gu, ffn1_w_down, mix_norm, w_in, a_q_norm, a_k_norm, b_q_norm, b_k_norm,
           b_lambda_q1, b_lambda_k1, b_lambda_q2, b_lambda_k2, b_subln, w_out, ffn2_norm, ffn2_w_gu,
           ffn2_w_down, out_norm):
    bsz, seq, d = x.shape
    depth = w_in.shape[0]
    d_ff = ffn1_w_down.shape[1]
    m = bsz * seq
    hd = HEAD_DIM
    a_heads = d // (2 * hd)
    a_kv = a_heads // A_GROUP
    b_vdim = 2 * hd
    b_heads = d // (2 * b_vdim)
    a_q, a_kvw = a_heads * hd, a_kv * hd
    b_qk, b_v = b_heads * 2 * hd, b_heads * b_vdim
    t = _tiles(m, seq, d_ff)
    q_scale = (hd ** -0.5) * LOG2E
    tab_a, tab_b = _rope_tables(seq)

    x = x.reshape(m, d)
    for l in range(depth):
        lambda_init = 0.8 - 0.6 * math.exp(-0.3 * l)
        row = lambda v: v[l].reshape(1, -1).astype(F32)
        w_gu1, w_d1 = ffn1_w_gu[l].astype(BF16), ffn1_w_down[l].astype(BF16)
        w_gu2, w_d2 = ffn2_w_gu[l].astype(BF16), ffn2_w_down[l].astype(BF16)
        w_o = w_out[l].astype(BF16)
        c1 = a_q + a_kvw
        c2 = c1 + a_kvw
        c3 = c2 + 2 * b_qk
        w_i = w_in[l]
        w_qk_a, w_v_a = _permute_heads(w_i[:, :c1], AXIAL_ORDER).astype(BF16), w_i[:, c1:c2].astype(BF16)
        w_qk_b, w_v_b = _permute_heads(w_i[:, c2:c3], PARTIAL_ORDER).astype(BF16), w_i[:, c3:].astype(BF16)

        x, h = _ffn(x, row(ffn1_norm), w_gu1, w_d1, row(mix_norm),
                    emit_norm=True, final_norm=False, tm=t["ffn1_tm"], tf=t["ffn_tf"])

        tqa, tqb, tka, tkb, ptm = t["a_tq"], t["b_tq"], t["a_tk"], t["b_tk"], t["proj_tm"]
        gain_a = jnp.concatenate([jnp.tile(_permute_heads(row(a_q_norm), AXIAL_ORDER) * q_scale, (1, a_heads)),
                                  jnp.tile(_permute_heads(row(a_k_norm), AXIAL_ORDER), (1, a_kv))], axis=-1)
        qa_t, ka, va_t = _proj(h, w_qk_a, w_v_a, gain_a, *tab_a, bsz=bsz, seq=seq, n_qc=a_heads, n_kc=a_kv,
                               per_group=A_GROUP, dv=hd, tq=tqa, tk=tka, tm=ptm)
        gain_b = jnp.concatenate([jnp.tile(_permute_heads(row(b_q_norm), PARTIAL_ORDER) * q_scale, (1, 2 * b_heads)),
                                  jnp.tile(_permute_heads(row(b_k_norm), PARTIAL_ORDER), (1, 2 * b_heads))], axis=-1)
        qb_t, kb, vb_t = _proj(h, w_qk_b, w_v_b, gain_b, *tab_b, bsz=bsz, seq=seq, n_qc=2 * b_heads, n_kc=2 * b_heads,
                               per_group=2, dv=b_vdim, tq=tqb, tk=tkb, tm=ptm)
        ka = ka.reshape(bsz, a_kv, 1, seq // tka, tka, hd)
        kb = kb.reshape(bsz, b_heads, 2, seq // tkb, tkb, hd)

        out_a = _attn(qa_t, ka, va_t, tq=tqa, tiles_per_step=t["attn_tiles"])
        out_b = _attn(qb_t, kb, vb_t, tq=tqb, tiles_per_step=t["attn_tiles"],
                      diff_args=(row(b_lambda_q1), row(b_lambda_k1), row(b_lambda_q2), row(b_lambda_k2), row(b_subln)),
                      lambda_init=lambda_init)

        x = _outproj(x, out_a.reshape(m, a_q), out_b.reshape(m, b_v), w_o, tm=t["out_tm"])
        x, = _ffn(x, row(ffn2_norm), w_gu2, w_d2, row(out_norm),
                  emit_norm=False, final_norm=True, tm=t["ffn2_tm"], tf=t["ffn_tf"])
    return x.reshape(bsz, seq, d)
```

```python
import functools
import math

import jax
import jax.numpy as jnp
from jax import lax
from jax.experimental import pallas as pl
from jax.experimental.pallas import tpu as pltpu

HEAD_DIM = 128
GRID_W = 64
EPS = 1e-6
A_ROPE_THETA = 10000.0
PARTIAL_ROPE_THETA = 500000.0
AXIAL_DIM = HEAD_DIM // 2
PARTIAL_ROPE_DIM = HEAD_DIM // 4
A_GROUP = 4
LANES = 128
LOG2E = math.log2(math.e)
VMEM_LIMIT_BYTES = 63 * 1024 * 1024
NEG_BIG = -1e30
L_MIN_SAFE = 1e-20

BF16 = jnp.bfloat16
F32 = jnp.float32


def _dot(a, b):
    return jnp.dot(a, b, preferred_element_type=F32)


def _rms(x, gain):
    r = lax.rsqrt(jnp.mean(x * x, axis=-1, keepdims=True) + EPS)
    return (x * r) * gain


def _params(*semantics):
    return pltpu.CompilerParams(dimension_semantics=semantics, vmem_limit_bytes=VMEM_LIMIT_BYTES)


def _ffn_kernel(x_ref, gain_ref, wg_ref, wu_ref, wd_ref, ngain_ref, o_ref, *rest, emit_norm, final_norm, single_step):
    if emit_norm:
        hn_ref, xn_ref = rest
    else:
        (xn_ref,) = rest
    j = pl.program_id(1)
    last = pl.num_programs(1) - 1

    def step(first, final):
        if first:
            x = x_ref[...]
            xn = _rms(x, gain_ref[...]).astype(BF16)
            xn_ref[...] = xn
        else:
            xn = xn_ref[...]
        g = _dot(xn, wg_ref[...])
        u = _dot(xn, wu_ref[...])
        act = (0.5 * g / (1.0 + jnp.exp(-g))) * u
        o = (x_ref[...] if first else o_ref[...]) + _dot(act.astype(BF16), wd_ref[...])
        if final:
            y = _rms(o, ngain_ref[...])
            if emit_norm:
                hn_ref[...] = y.astype(BF16)
            if final_norm:
                o = y
        o_ref[...] = o

    if single_step:
        step(True, True)
    else:
        pl.when(j == 0)(lambda: step(True, False))
        pl.when(jnp.logical_and(j > 0, j < last))(lambda: step(False, False))
        pl.when(j == last)(lambda: step(False, True))


def _ffn(x, gain, w_gu, w_down, next_gain, *, emit_norm, final_norm, tm, tf):
    m, d = x.shape
    d_ff = w_down.shape[0]
    nf = d_ff // tf
    assert m % tm == 0 and d_ff % tf == 0
    out_shape = [jax.ShapeDtypeStruct((m, d), F32)]
    out_specs = [pl.BlockSpec((tm, d), lambda i, j: (i, 0))]
    if emit_norm:
        out_shape.append(jax.ShapeDtypeStruct((m, d), BF16))
        out_specs.append(pl.BlockSpec((tm, d), lambda i, j: (i, 0)))
    res = pl.pallas_call(
        functools.partial(_ffn_kernel, emit_norm=emit_norm, final_norm=final_norm, single_step=nf == 1),
        grid=(m // tm, nf),
        in_specs=[
            pl.BlockSpec((tm, d), lambda i, j: (i, 0)),
            pl.BlockSpec((1, d), lambda i, j: (0, 0)),
            pl.BlockSpec((d, tf), lambda i, j: (0, j)),
            pl.BlockSpec((d, tf), lambda i, j: (0, j + nf)),
            pl.BlockSpec((tf, d), lambda i, j: (j, 0)),
            pl.BlockSpec((1, d), lambda i, j: (0, 0)),
        ],
        out_specs=out_specs,
        out_shape=out_shape,
        scratch_shapes=[pltpu.VMEM((tm, d), BF16)],
        compiler_params=_params("parallel", "arbitrary"),
        name="ffn",
    )(x, gain, w_gu, w_gu, w_down, next_gain)
    return res


PROJ_COLS = 2 * LANES


def _proj_kernel(h_ref, wqk_ref, wv_ref, gain_ref, *refs, n_qc, n_kc, per_group, tq, axial):
    n_tab = 4 if axial else 2
    tab_refs, (q_ref, k_ref, v_ref, acc_a, acc_b) = refs[:n_tab], refs[n_tab:]
    i = pl.program_id(0)
    tm = h_ref.shape[0]
    dv = v_ref.shape[1]
    v0 = wqk_ref.shape[1]
    width = v0 + wv_ref.shape[1]

    @pl.when(i == 0)
    def _():
        acc_b[...] = jnp.zeros(acc_b.shape, F32)

    def step(mm_ref, ep_ref):
        for w_ref, base in ((wqk_ref, 0), (wv_ref, v0)):
            for c0 in range(0, w_ref.shape[1], PROJ_COLS):
                cols = min(PROJ_COLS, w_ref.shape[1] - c0)
                mm_ref[:, base + c0:base + c0 + cols] = _dot(h_ref[...], w_ref[:, c0:c0 + cols])
        if axial:
            rcos_ref, rsin_ref, ccos_ref, csin_ref = tab_refs
            cos = jnp.concatenate([rcos_ref[r:r + 1, :] + ccos_ref[...] for r in range(tm // GRID_W)], axis=0)
            sin = jnp.concatenate([rsin_ref[r:r + 1, :] + csin_ref[...] for r in range(tm // GRID_W)], axis=0)
        else:
            cos, sin = tab_refs[0][...], tab_refs[1][...]
        for c in range(n_qc + n_kc):
            sl = slice(c * LANES, (c + 1) * LANES)
            y = _rms(ep_ref[:, sl], gain_ref[:, sl])
            y = y * cos + pltpu.roll(y, LANES // 2, 1) * sin
            if c < n_qc:
                g, r = divmod(c, per_group)
                yt = y.T.astype(BF16)
                for j in range(tm // tq):
                    q_ref[g, j, :, r * tq:(r + 1) * tq] = yt[:, j * tq:(j + 1) * tq]
            else:
                k_ref[c - n_qc] = y.astype(BF16)
        for g in range((width - v0) // dv):
            v_ref[g] = ep_ref[:, v0 + g * dv:v0 + (g + 1) * dv].T.astype(BF16)

    @pl.when(i % 2 == 0)
    def _():
        step(acc_a, acc_b)

    @pl.when(i % 2 == 1)
    def _():
        step(acc_b, acc_a)


def _proj(h, w_qk, w_v, gain, tables, *, bsz, seq, n_qc, n_kc, per_group, dv, tq, tk, tm):
    m, d = h.shape
    qk_w, v_w = w_qk.shape[1], w_v.shape[1]
    width = qk_w + v_w
    groups, v_groups = n_qc // per_group, v_w // dv
    nt, per_chunk, n = seq // tm, tk // tm, m // tm
    assert tm % tq == 0 and tk % tm == 0 and seq % tk == 0 and v_w % dv == 0 and qk_w == (n_qc + n_kc) * LANES

    def cur(i):
        return jnp.minimum(i, n - 1)

    def prev(i):
        return jnp.maximum(i - 1, 0)

    axial = len(tables) == 4
    if axial:
        assert tm % GRID_W == 0
        table_specs = ([pl.BlockSpec((tm // GRID_W, LANES), lambda i: (prev(i) % nt, 0))] * 2
                       + [pl.BlockSpec((GRID_W, LANES), lambda i: (0, 0))] * 2)
    else:
        table_specs = [pl.BlockSpec((tm, LANES), lambda i: (prev(i) % nt, 0))] * 2
    return pl.pallas_call(
        functools.partial(_proj_kernel, n_qc=n_qc, n_kc=n_kc, per_group=per_group, tq=tq, axial=axial),
        grid=(n + 1,),
        in_specs=[
            pl.BlockSpec((tm, d), lambda i: (cur(i), 0)),
            pl.BlockSpec((d, qk_w), lambda i: (0, 0)),
            pl.BlockSpec((d, v_w), lambda i: (0, 0)),
            pl.BlockSpec((1, qk_w), lambda i: (0, 0)),
            *table_specs,
        ],
        out_specs=[
            pl.BlockSpec((None, groups, tm // tq, LANES, per_group * tq),
                         lambda i: (prev(i) // nt, 0, prev(i) % nt, 0, 0)),
            pl.BlockSpec((None, n_kc, tm, LANES), lambda i: (prev(i) // nt, 0, prev(i) % nt, 0)),
            pl.BlockSpec((None, v_groups, None, dv, tm),
                         lambda i: (prev(i) // nt, 0, (prev(i) % nt) // per_chunk, 0, (prev(i) % nt) % per_chunk)),
        ],
        out_shape=[
            jax.ShapeDtypeStruct((bsz, groups, seq // tq, LANES, per_group * tq), BF16),
            jax.ShapeDtypeStruct((bsz, n_kc, seq, LANES), BF16),
            jax.ShapeDtypeStruct((bsz, v_groups, seq // tk, dv, tk), BF16),
        ],
        scratch_shapes=[pltpu.VMEM((tm, width), F32), pltpu.VMEM((tm, width), F32)],
        compiler_params=_params("arbitrary"),
        name="proj",
    )(h, w_qk, w_v, gain, *tables)


def _attn_kernel(*refs, tq, diff, lambda_init):
    if diff:
        q_ref, k_ref, v_ref, lq1_ref, lk1_ref, lq2_ref, lk2_ref, subln_ref, o_ref, acc_ref, m_ref, l_ref, l8_ref, kmax_ref = refs
    else:
        q_ref, k_ref, v_ref, o_ref, acc_ref, m_ref, l_ref, l8_ref, kmax_ref = refs
    n_k, n_chunks, tk = k_ref.shape[0], k_ref.shape[1], k_ref.shape[2]
    w = q_ref.shape[1]
    wk = w // n_k

    def scores(c):
        parts = [_dot(k_ref[ks, c], q_ref[:, ks * wk:(ks + 1) * wk]) for ks in range(n_k)]
        return parts[0] if n_k == 1 else jnp.concatenate(parts, axis=1)

    @pl.when(pl.program_id(2) == 0)
    def _():
        for ks in range(n_k):
            def kbody(c, best):
                kf = k_ref[ks, c].astype(F32)
                return jnp.maximum(best, jnp.max(jnp.sum(kf * kf, axis=-1, keepdims=True), axis=0, keepdims=True))
            k2 = lax.fori_loop(0, n_chunks, kbody, jnp.zeros((1, 1), F32))
            kmax_ref[ks] = jnp.broadcast_to(jnp.sqrt(k2), (1, LANES))

    qf = q_ref[...].astype(F32)
    qn = jnp.sqrt(jnp.sum(qf * qf, axis=0, keepdims=True))
    for ks in range(n_k):
        m_ref[:, ks * wk:(ks + 1) * wk] = qn[:, ks * wk:(ks + 1) * wk] * kmax_ref[ks, :, 0:1]
    acc_ref[...] = jnp.zeros(acc_ref.shape, F32)
    l8_ref[...] = jnp.zeros(l8_ref.shape, F32)

    def fast_body(c, carry):
        p = jnp.exp2(scores(c) - m_ref[...])
        l8_ref[...] += jnp.sum(p.reshape(tk // 8, 8, w), axis=0)
        acc_ref[...] += _dot(v_ref[c], p.astype(BF16))
        return carry

    lax.fori_loop(0, n_chunks, fast_body, 0, unroll=True)
    l_ref[...] = jnp.sum(l8_ref[...], axis=0, keepdims=True)
    underflow = jnp.min(l_ref[...]) < L_MIN_SAFE

    @pl.when(underflow)
    def _():
        acc_ref[...] = jnp.zeros(acc_ref.shape, F32)
        m_ref[...] = jnp.full(m_ref.shape, NEG_BIG, F32)
        l_ref[...] = jnp.zeros(l_ref.shape, F32)

        def body(c, carry):
            s = scores(c)
            m_old = m_ref[...]
            m_new = jnp.maximum(m_old, jnp.max(s, axis=0, keepdims=True))
            alpha = jnp.exp2(m_old - m_new)
            p = jnp.exp2(s - m_new)
            l_ref[...] = alpha * l_ref[...] + jnp.sum(p, axis=0, keepdims=True)
            acc_ref[...] = alpha * acc_ref[...] + _dot(v_ref[c], p.astype(BF16))
            m_ref[...] = m_new
            return carry

        lax.fori_loop(0, n_chunks, body, 0)

    o = acc_ref[...] * (1.0 / l_ref[...])
    if not diff:
        for r in range(w // tq):
            o_ref[:, r * LANES:(r + 1) * LANES] = o[:, r * tq:(r + 1) * tq].T.astype(BF16)
    else:
        lam = (jnp.exp(jnp.sum(lq1_ref[...] * lk1_ref[...], axis=-1, keepdims=True))
               - jnp.exp(jnp.sum(lq2_ref[...] * lk2_ref[...], axis=-1, keepdims=True))
               + lambda_init)
        d = o[:, :tq] - lam * o[:, tq:]
        y = _rms(d.T, subln_ref[...]) * (1.0 - lambda_init)
        o_ref[...] = y.astype(BF16)


def _attn(qt, k, vt, *, tq, diff_args=None, lambda_init=0.0):
    b, g, nqt, hd, w = qt.shape
    n_k, nc, tk = k.shape[2], k.shape[3], k.shape[4]
    dv = vt.shape[3]
    diff = diff_args is not None
    width = dv if diff else (w // tq) * dv
    in_specs = [
        pl.BlockSpec((None, None, None, hd, w), lambda bi, gi, qi: (bi, gi, qi, 0, 0)),
        pl.BlockSpec((None, None, n_k, nc, tk, hd), lambda bi, gi, qi: (bi, gi, 0, 0, 0, 0),
                     pipeline_mode=pl.Buffered(1)),
        pl.BlockSpec((None, None, nc, dv, tk), lambda bi, gi, qi: (bi, gi, 0, 0, 0),
                     pipeline_mode=pl.Buffered(1)),
    ]
    args = [qt, k, vt]
    if diff:
        in_specs += [pl.BlockSpec((1, hd), lambda bi, gi, qi: (0, 0))] * 4
        in_specs += [pl.BlockSpec((1, dv), lambda bi, gi, qi: (0, 0))]
        args += list(diff_args)
    return pl.pallas_call(
        functools.partial(_attn_kernel, tq=tq, diff=diff, lambda_init=lambda_init),
        grid=(b, g, nqt),
        in_specs=in_specs,
        out_specs=pl.BlockSpec((None, tq, width), lambda bi, gi, qi: (bi, qi, gi)),
        out_shape=jax.ShapeDtypeStruct((b, nqt * tq, g * width), BF16),
        scratch_shapes=[
            pltpu.VMEM((dv, w), F32),
            pltpu.VMEM((1, w), F32),
            pltpu.VMEM((1, w), F32),
            pltpu.VMEM((8, w), F32),
            pltpu.VMEM((n_k, 1, LANES), F32),
        ],
        compiler_params=_params("parallel", "parallel", "arbitrary"),
        name="attn_diff" if diff else "attn_gqa",
    )(*args)


def _outproj_kernel(x_ref, a_ref, b_ref, wa_ref, wb_ref, o_ref):
    o_ref[...] = x_ref[...] + _dot(a_ref[...], wa_ref[...]) + _dot(b_ref[...], wb_ref[...])


def _outproj(x, a, b, w_out, *, tm):
    m, d = x.shape
    da, db = a.shape[1], b.shape[1]
    assert da == db and w_out.shape[0] == da + db
    return pl.pallas_call(
        _outproj_kernel,
        grid=(m // tm,),
        in_specs=[
            pl.BlockSpec((tm, d), lambda i: (i, 0)),
            pl.BlockSpec((tm, da), lambda i: (i, 0)),
            pl.BlockSpec((tm, db), lambda i: (i, 0)),
            pl.BlockSpec((da, d), lambda i: (0, 0)),
            pl.BlockSpec((db, d), lambda i: (1, 0)),
        ],
        out_specs=pl.BlockSpec((tm, d), lambda i: (i, 0)),
        out_shape=jax.ShapeDtypeStruct((m, d), F32),
        compiler_params=_params("parallel"),
        name="outproj",
    )(x, a, b, w_out, w_out)


def _rope_cs(pos, dim, theta):
    inv_freq = theta ** (-jnp.arange(0, dim, 2, dtype=F32) / dim)
    ang = pos[:, None] * inv_freq[None, :]
    return jnp.cos(ang), jnp.sin(ang)


AXIAL_ORDER = (2, 2)
PARTIAL_ORDER = (4, 2)


def _permute_heads(a, order):
    n1, n2 = order
    lead = a.shape[:-1]
    a5 = a.reshape(lead + (a.shape[-1] // HEAD_DIM, n1, n2, HEAD_DIM // (n1 * n2)))
    return jnp.swapaxes(a5, -3, -2).reshape(a.shape)


def _rope_tables(seq):
    rows = seq // GRID_W
    rc, rs = _rope_cs(jnp.arange(rows, dtype=F32), AXIAL_DIM, A_ROPE_THETA)
    cc, cs = _rope_cs(jnp.arange(GRID_W, dtype=F32), AXIAL_DIM, A_ROPE_THETA)
    pc, ps = _rope_cs(jnp.arange(seq, dtype=F32), PARTIAL_ROPE_DIM, PARTIAL_ROPE_THETA)
    zr, zc = jnp.zeros_like(rc), jnp.zeros_like(cc)
    tab_a = (jnp.concatenate([rc, zr, rc, zr], axis=-1), jnp.concatenate([-rs, zr, rs, zr], axis=-1),
             jnp.concatenate([zc, cc, zc, cc], axis=-1), jnp.concatenate([zc, -cs, zc, cs], axis=-1))
    rest = HEAD_DIM // 2 - PARTIAL_ROPE_DIM // 2
    one, zero = jnp.ones((seq, rest), F32), jnp.zeros((seq, rest), F32)
    cos_b = jnp.concatenate([pc, one, pc, one], axis=-1)
    sin_b = jnp.concatenate([-ps, zero, ps, zero], axis=-1)
    return tab_a, (cos_b, sin_b)


def _tiles(m, seq, d_ff):
    def fit(n, t):
        t = min(t, n)
        while n % t:
            t //= 2
        return t
    return dict(
        ffn1_tm=fit(m, 1024), ffn2_tm=fit(m, 1024), ffn_tf=fit(d_ff, 512),
        proj_tm=fit(seq, 512), out_tm=fit(m, 512),
        a_tq=fit(seq, 256), b_tq=fit(seq, 512), a_tk=fit(seq, 4096), b_tk=fit(seq, 4096),
    )


def kernel(x, ffn1_norm, ffn1_w_gu, ffn1_w_down, mix_norm, w_in, a_q_norm, a_k_norm, b_q_norm, b_k_norm,
           b_lambda_q1, b_lambda_k1, b_lambda_q2, b_lambda_k2, b_subln, w_out, ffn2_norm, ffn2_w_gu,
           ffn2_w_down, out_norm):
    bsz, seq, d = x.shape
    depth = w_in.shape[0]
    d_ff = ffn1_w_down.shape[1]
    m = bsz * seq
    hd = HEAD_DIM
    a_heads = d // (2 * hd)
    a_kv = a_heads // A_GROUP
    b_vdim = 2 * hd
    b_heads = d // (2 * b_vdim)
    a_q, a_kvw = a_heads * hd, a_kv * hd
    b_qk, b_v = b_heads * 2 * hd, b_heads * b_vdim
    t = _tiles(m, seq, d_ff)
    q_scale = (hd ** -0.5) * LOG2E
    tab_a, tab_b = _rope_tables(seq)

    x = x.reshape(m, d)
    for l in range(depth):
        lambda_init = 0.8 - 0.6 * math.exp(-0.3 * l)
        row = lambda v: v[l].reshape(1, -1).astype(F32)
        w_gu1, w_d1 = ffn1_w_gu[l].astype(BF16), ffn1_w_down[l].astype(BF16)
        w_gu2, w_d2 = ffn2_w_gu[l].astype(BF16), ffn2_w_down[l].astype(BF16)
        w_o = w_out[l].astype(BF16)
        c1 = a_q + a_kvw
        c2 = c1 + a_kvw
        c3 = c2 + 2 * b_qk
        w_i = w_in[l]
        w_qk_a, w_v_a = _permute_heads(w_i[:, :c1], AXIAL_ORDER).astype(BF16), w_i[:, c1:c2].astype(BF16)
        w_qk_b, w_v_b = _permute_heads(w_i[:, c2:c3], PARTIAL_ORDER).astype(BF16), w_i[:, c3:].astype(BF16)

        x, h = _ffn(x, row(ffn1_norm), w_gu1, w_d1, row(mix_norm),
                    emit_norm=True, final_norm=False, tm=t["ffn1_tm"], tf=t["ffn_tf"])

        tqa, tqb, tka, tkb, ptm = t["a_tq"], t["b_tq"], t["a_tk"], t["b_tk"], t["proj_tm"]
        gain_a = jnp.concatenate([jnp.tile(_permute_heads(row(a_q_norm), AXIAL_ORDER) * q_scale, (1, a_heads)),
                                  jnp.tile(_permute_heads(row(a_k_norm), AXIAL_ORDER), (1, a_kv))], axis=-1)
        qa_t, ka, va_t = _proj(h, w_qk_a, w_v_a, gain_a, tab_a, bsz=bsz, seq=seq, n_qc=a_heads, n_kc=a_kv,
                               per_group=A_GROUP, dv=hd, tq=tqa, tk=tka, tm=ptm)
        gain_b = jnp.concatenate([jnp.tile(_permute_heads(row(b_q_norm), PARTIAL_ORDER) * q_scale, (1, 2 * b_heads)),
                                  jnp.tile(_permute_heads(row(b_k_norm), PARTIAL_ORDER), (1, 2 * b_heads))], axis=-1)
        qb_t, kb, vb_t = _proj(h, w_qk_b, w_v_b, gain_b, tab_b, bsz=bsz, seq=seq, n_qc=2 * b_heads, n_kc=2 * b_heads,
                               per_group=2, dv=b_vdim, tq=tqb, tk=tkb, tm=ptm)
        ka = ka.reshape(bsz, a_kv, 1, seq // tka, tka, hd)
        kb = kb.reshape(bsz, b_heads, 2, seq // tkb, tkb, hd)

        out_a = _attn(qa_t, ka, va_t, tq=tqa)
        out_b = _attn(qb_t, kb, vb_t, tq=tqb,
                      diff_args=(row(b_lambda_q1), row(b_lambda_k1), row(b_lambda_q2), row(b_lambda_k2), row(b_subln)),
                      lambda_init=lambda_init)

        x = _outproj(x, out_a.reshape(m, a_q), out_b.reshape(m, b_v), w_o, tm=t["out_tm"])
        x, = _ffn(x, row(ffn2_norm), w_gu2, w_d2, row(out_norm),
                  emit_norm=False, final_norm=True, tm=t["ffn2_tm"], tf=t["ffn_tf"])
    return x.reshape(bsz, seq, d)
```

```python
import functools
import math

import jax
import jax.numpy as jnp
from jax import lax
from jax.experimental import pallas as pl
from jax.experimental.pallas import tpu as pltpu

HEAD_DIM = 128
GRID_W = 64
EPS = 1e-6
A_ROPE_THETA = 10000.0
PARTIAL_ROPE_THETA = 500000.0
AXIAL_DIM = HEAD_DIM // 2
PARTIAL_ROPE_DIM = HEAD_DIM // 4
A_GROUP = 4
LANES = 128
LOG2E = math.log2(math.e)
VMEM_LIMIT_BYTES = 63 * 1024 * 1024
NEG_BIG = -1e30
L_MIN_SAFE = 1e-20

BF16 = jnp.bfloat16
F32 = jnp.float32


def _dot(a, b):
    return jnp.dot(a, b, preferred_element_type=F32)


def _rms(x, gain):
    r = lax.rsqrt(jnp.mean(x * x, axis=-1, keepdims=True) + EPS)
    return (x * r) * gain


def _params(*semantics):
    return pltpu.CompilerParams(dimension_semantics=semantics, vmem_limit_bytes=VMEM_LIMIT_BYTES)


def _ffn_kernel(x_ref, gain_ref, wg_ref, wu_ref, wd_ref, ngain_ref, o_ref, *rest, emit_norm, final_norm, single_step):
    if emit_norm:
        hn_ref, xn_ref = rest
    else:
        (xn_ref,) = rest
    j = pl.program_id(1)
    last = pl.num_programs(1) - 1

    def step(first, final):
        if first:
            x = x_ref[...]
            xn = _rms(x, gain_ref[...]).astype(BF16)
            xn_ref[...] = xn
        else:
            xn = xn_ref[...]
        g = _dot(xn, wg_ref[...])
        u = _dot(xn, wu_ref[...])
        act = (0.5 * g / (1.0 + jnp.exp(-g))) * u
        o = (x_ref[...] if first else o_ref[...]) + _dot(act.astype(BF16), wd_ref[...])
        if final:
            y = _rms(o, ngain_ref[...])
            if emit_norm:
                hn_ref[...] = y.astype(BF16)
            if final_norm:
                o = y
        o_ref[...] = o

    if single_step:
        step(True, True)
    else:
        pl.when(j == 0)(lambda: step(True, False))
        pl.when(jnp.logical_and(j > 0, j < last))(lambda: step(False, False))
        pl.when(j == last)(lambda: step(False, True))


def _ffn(x, gain, w_gu, w_down, next_gain, *, emit_norm, final_norm, tm, tf):
    m, d = x.shape
    d_ff = w_down.shape[0]
    nf = d_ff // tf
    assert m % tm == 0 and d_ff % tf == 0
    out_shape = [jax.ShapeDtypeStruct((m, d), F32)]
    out_specs = [pl.BlockSpec((tm, d), lambda i, j: (i, 0))]
    if emit_norm:
        out_shape.append(jax.ShapeDtypeStruct((m, d), BF16))
        out_specs.append(pl.BlockSpec((tm, d), lambda i, j: (i, 0)))
    res = pl.pallas_call(
        functools.partial(_ffn_kernel, emit_norm=emit_norm, final_norm=final_norm, single_step=nf == 1),
        grid=(m // tm, nf),
        in_specs=[
            pl.BlockSpec((tm, d), lambda i, j: (i, 0)),
            pl.BlockSpec((1, d), lambda i, j: (0, 0)),
            pl.BlockSpec((d, tf), lambda i, j: (0, j)),
            pl.BlockSpec((d, tf), lambda i, j: (0, j + nf)),
            pl.BlockSpec((tf, d), lambda i, j: (j, 0)),
            pl.BlockSpec((1, d), lambda i, j: (0, 0)),
        ],
        out_specs=out_specs,
        out_shape=out_shape,
        scratch_shapes=[pltpu.VMEM((tm, d), BF16)],
        compiler_params=_params("parallel", "arbitrary"),
        name="ffn",
    )(x, gain, w_gu, w_gu, w_down, next_gain)
    return res


PROJ_COLS = 2 * LANES


def _proj_kernel(h_ref, wqk_ref, wv_ref, gain_ref, *refs, n_qc, n_kc, per_group, tq):
    tab_refs, (q_ref, k_ref, v_ref, acc_a, acc_b) = refs[:4], refs[4:]
    i = pl.program_id(0)
    tm = h_ref.shape[0]
    dv = v_ref.shape[1]
    v0 = wqk_ref.shape[1]
    width = v0 + wv_ref.shape[1]

    @pl.when(i == 0)
    def _():
        acc_b[...] = jnp.zeros(acc_b.shape, F32)

    def step(mm_ref, ep_ref):
        for w_ref, base in ((wqk_ref, 0), (wv_ref, v0)):
            for c0 in range(0, w_ref.shape[1], PROJ_COLS):
                cols = min(PROJ_COLS, w_ref.shape[1] - c0)
                mm_ref[:, base + c0:base + c0 + cols] = _dot(h_ref[...], w_ref[:, c0:c0 + cols])
        rcos_ref, rsin_ref, ccos_ref, csin_ref = tab_refs
        ccos, csin = ccos_ref[...], csin_ref[...]
        cos_rows, sin_rows = [], []
        for r in range(tm // GRID_W):
            rc, rs = rcos_ref[r:r + 1, :], rsin_ref[r:r + 1, :]
            cos_rows.append(rc * ccos - rs * csin)
            sin_rows.append(rs * ccos + rc * csin)
        cos, sin = jnp.concatenate(cos_rows, axis=0), jnp.concatenate(sin_rows, axis=0)
        for c in range(n_qc + n_kc):
            sl = slice(c * LANES, (c + 1) * LANES)
            y = _rms(ep_ref[:, sl], gain_ref[:, sl])
            y = y * cos + pltpu.roll(y, LANES // 2, 1) * sin
            if c < n_qc:
                g, r = divmod(c, per_group)
                yt = y.T.astype(BF16)
                for j in range(tm // tq):
                    q_ref[g, j, :, r * tq:(r + 1) * tq] = yt[:, j * tq:(j + 1) * tq]
            else:
                k_ref[c - n_qc] = y.astype(BF16)
        for g in range((width - v0) // dv):
            v_ref[g] = ep_ref[:, v0 + g * dv:v0 + (g + 1) * dv].T.astype(BF16)

    @pl.when(i % 2 == 0)
    def _():
        step(acc_a, acc_b)

    @pl.when(i % 2 == 1)
    def _():
        step(acc_b, acc_a)


def _proj(h, w_qk, w_v, gain, tables, *, bsz, seq, n_qc, n_kc, per_group, dv, tq, tk, tm):
    m, d = h.shape
    qk_w, v_w = w_qk.shape[1], w_v.shape[1]
    width = qk_w + v_w
    groups, v_groups = n_qc // per_group, v_w // dv
    nt, per_chunk, n = seq // tm, tk // tm, m // tm
    assert tm % tq == 0 and tk % tm == 0 and seq % tk == 0 and v_w % dv == 0 and qk_w == (n_qc + n_kc) * LANES

    def cur(i):
        return jnp.minimum(i, n - 1)

    def prev(i):
        return jnp.maximum(i - 1, 0)

    assert tm % GRID_W == 0 and len(tables) == 4
    table_specs = ([pl.BlockSpec((tm // GRID_W, LANES), lambda i: (prev(i) % nt, 0))] * 2
                   + [pl.BlockSpec((GRID_W, LANES), lambda i: (0, 0))] * 2)
    return pl.pallas_call(
        functools.partial(_proj_kernel, n_qc=n_qc, n_kc=n_kc, per_group=per_group, tq=tq),
        grid=(n + 1,),
        in_specs=[
            pl.BlockSpec((tm, d), lambda i: (cur(i), 0)),
            pl.BlockSpec((d, qk_w), lambda i: (0, 0)),
            pl.BlockSpec((d, v_w), lambda i: (0, 0)),
            pl.BlockSpec((1, qk_w), lambda i: (0, 0)),
            *table_specs,
        ],
        out_specs=[
            pl.BlockSpec((None, groups, tm // tq, LANES, per_group * tq),
                         lambda i: (prev(i) // nt, 0, prev(i) % nt, 0, 0)),
            pl.BlockSpec((None, n_kc, tm, LANES), lambda i: (prev(i) // nt, 0, prev(i) % nt, 0)),
            pl.BlockSpec((None, v_groups, None, dv, tm),
                         lambda i: (prev(i) // nt, 0, (prev(i) % nt) // per_chunk, 0, (prev(i) % nt) % per_chunk)),
        ],
        out_shape=[
            jax.ShapeDtypeStruct((bsz, groups, seq // tq, LANES, per_group * tq), BF16),
            jax.ShapeDtypeStruct((bsz, n_kc, seq, LANES), BF16),
            jax.ShapeDtypeStruct((bsz, v_groups, seq // tk, dv, tk), BF16),
        ],
        scratch_shapes=[pltpu.VMEM((tm, width), F32), pltpu.VMEM((tm, width), F32)],
        compiler_params=_params("arbitrary"),
        name="proj",
    )(h, w_qk, w_v, gain, *tables)


def _attn_kernel(*refs, tq, diff, lambda_init):
    if diff:
        q_ref, k_ref, v_ref, lq1_ref, lk1_ref, lq2_ref, lk2_ref, subln_ref, o_ref, acc_ref, m_ref, l_ref, l8_ref, kmax_ref = refs
    else:
        q_ref, k_ref, v_ref, o_ref, acc_ref, m_ref, l_ref, l8_ref, kmax_ref = refs
    n_k, n_chunks, tk = k_ref.shape[0], k_ref.shape[1], k_ref.shape[2]
    w = q_ref.shape[1]
    wk = w // n_k

    def scores(c):
        parts = [_dot(k_ref[ks, c], q_ref[:, ks * wk:(ks + 1) * wk]) for ks in range(n_k)]
        return parts[0] if n_k == 1 else jnp.concatenate(parts, axis=1)

    @pl.when(pl.program_id(2) == 0)
    def _():
        for ks in range(n_k):
            def kbody(c, best):
                kf = k_ref[ks, c].astype(F32)
                return jnp.maximum(best, jnp.max(jnp.sum(kf * kf, axis=-1, keepdims=True), axis=0, keepdims=True))
            k2 = lax.fori_loop(0, n_chunks, kbody, jnp.zeros((1, 1), F32))
            kmax_ref[ks] = jnp.broadcast_to(jnp.sqrt(k2), (1, LANES))

    qf = q_ref[...].astype(F32)
    qn = jnp.sqrt(jnp.sum(qf * qf, axis=0, keepdims=True))
    for ks in range(n_k):
        m_ref[:, ks * wk:(ks + 1) * wk] = qn[:, ks * wk:(ks + 1) * wk] * kmax_ref[ks, :, 0:1]
    acc_ref[...] = jnp.zeros(acc_ref.shape, F32)
    l8_ref[...] = jnp.zeros(l8_ref.shape, F32)

    def fast_body(c, carry):
        p = jnp.exp2(scores(c) - m_ref[...])
        l8_ref[...] += jnp.sum(p.reshape(tk // 8, 8, w), axis=0)
        acc_ref[...] += _dot(v_ref[c], p.astype(BF16))
        return carry

    lax.fori_loop(0, n_chunks, fast_body, 0, unroll=True)
    l_ref[...] = jnp.sum(l8_ref[...], axis=0, keepdims=True)
    underflow = jnp.min(l_ref[...]) < L_MIN_SAFE

    @pl.when(underflow)
    def _():
        acc_ref[...] = jnp.zeros(acc_ref.shape, F32)
        m_ref[...] = jnp.full(m_ref.shape, NEG_BIG, F32)
        l_ref[...] = jnp.zeros(l_ref.shape, F32)

        def body(c, carry):
            s = scores(c)
            m_old = m_ref[...]
            m_new = jnp.maximum(m_old, jnp.max(s, axis=0, keepdims=True))
            alpha = jnp.exp2(m_old - m_new)
            p = jnp.exp2(s - m_new)
            l_ref[...] = alpha * l_ref[...] + jnp.sum(p, axis=0, keepdims=True)
            acc_ref[...] = alpha * acc_ref[...] + _dot(v_ref[c], p.astype(BF16))
            m_ref[...] = m_new
            return carry

        lax.fori_loop(0, n_chunks, body, 0)

    o = acc_ref[...] * (1.0 / l_ref[...])
    if not diff:
        for r in range(w // tq):
            o_ref[:, r * LANES:(r + 1) * LANES] = o[:, r * tq:(r + 1) * tq].T.astype(BF16)
    else:
        lam = (jnp.exp(jnp.sum(lq1_ref[...] * lk1_ref[...], axis=-1, keepdims=True))
               - jnp.exp(jnp.sum(lq2_ref[...] * lk2_ref[...], axis=-1, keepdims=True))
               + lambda_init)
        d = o[:, :tq] - lam * o[:, tq:]
        y = _rms(d.T, subln_ref[...]) * (1.0 - lambda_init)
        o_ref[...] = y.astype(BF16)


def _attn(qt, k, vt, *, tq, diff_args=None, lambda_init=0.0):
    b, g, nqt, hd, w = qt.shape
    n_k, nc, tk = k.shape[2], k.shape[3], k.shape[4]
    dv = vt.shape[3]
    diff = diff_args is not None
    width = dv if diff else (w // tq) * dv
    in_specs = [
        pl.BlockSpec((None, None, None, hd, w), lambda bi, gi, qi: (bi, gi, qi, 0, 0)),
        pl.BlockSpec((None, None, n_k, nc, tk, hd), lambda bi, gi, qi: (bi, gi, 0, 0, 0, 0),
                     pipeline_mode=pl.Buffered(1)),
        pl.BlockSpec((None, None, nc, dv, tk), lambda bi, gi, qi: (bi, gi, 0, 0, 0),
                     pipeline_mode=pl.Buffered(1)),
    ]
    args = [qt, k, vt]
    if diff:
        in_specs += [pl.BlockSpec((1, hd), lambda bi, gi, qi: (0, 0))] * 4
        in_specs += [pl.BlockSpec((1, dv), lambda bi, gi, qi: (0, 0))]
        args += list(diff_args)
    return pl.pallas_call(
        functools.partial(_attn_kernel, tq=tq, diff=diff, lambda_init=lambda_init),
        grid=(b, g, nqt),
        in_specs=in_specs,
        out_specs=pl.BlockSpec((None, tq, width), lambda bi, gi, qi: (bi, qi, gi)),
        out_shape=jax.ShapeDtypeStruct((b, nqt * tq, g * width), BF16),
        scratch_shapes=[
            pltpu.VMEM((dv, w), F32),
            pltpu.VMEM((1, w), F32),
            pltpu.VMEM((1, w), F32),
            pltpu.VMEM((8, w), F32),
            pltpu.VMEM((n_k, 1, LANES), F32),
        ],
        compiler_params=_params("parallel", "parallel", "arbitrary"),
        name="attn_diff" if diff else "attn_gqa",
    )(*args)


def _outproj_kernel(x_ref, a_ref, b_ref, wa_ref, wb_ref, o_ref):
    o_ref[...] = x_ref[...] + _dot(a_ref[...], wa_ref[...]) + _dot(b_ref[...], wb_ref[...])


def _outproj(x, a, b, w_out, *, tm):
    m, d = x.shape
    da, db = a.shape[1], b.shape[1]
    assert da == db and w_out.shape[0] == da + db
    return pl.pallas_call(
        _outproj_kernel,
        grid=(m // tm,),
        in_specs=[
            pl.BlockSpec((tm, d), lambda i: (i, 0)),
            pl.BlockSpec((tm, da), lambda i: (i, 0)),
            pl.BlockSpec((tm, db), lambda i: (i, 0)),
            pl.BlockSpec((da, d), lambda i: (0, 0)),
            pl.BlockSpec((db, d), lambda i: (1, 0)),
        ],
        out_specs=pl.BlockSpec((tm, d), lambda i: (i, 0)),
        out_shape=jax.ShapeDtypeStruct((m, d), F32),
        compiler_params=_params("parallel"),
        name="outproj",
    )(x, a, b, w_out, w_out)


def _rope_cs(pos, dim, theta):
    inv_freq = theta ** (-jnp.arange(0, dim, 2, dtype=F32) / dim)
    ang = pos[:, None] * inv_freq[None, :]
    return jnp.cos(ang), jnp.sin(ang)


AXIAL_ORDER = (2, 2)
PARTIAL_ORDER = (4, 2)


def _permute_heads(a, order):
    n1, n2 = order
    lead = a.shape[:-1]
    a5 = a.reshape(lead + (a.shape[-1] // HEAD_DIM, n1, n2, HEAD_DIM // (n1 * n2)))
    return jnp.swapaxes(a5, -3, -2).reshape(a.shape)


def _rope_tables(seq):
    rows = seq // GRID_W
    row, col = jnp.arange(rows, dtype=F32), jnp.arange(GRID_W, dtype=F32)

    def place(pieces_x1, n):
        def half(sign, kind):
            out = []
            for p, width in pieces_x1:
                if p is None:
                    out.append(jnp.ones((n, width), F32) if kind == 0 else jnp.zeros((n, width), F32))
                else:
                    out.append(p[0] if kind == 0 else sign * p[1])
            return out
        return (jnp.concatenate(half(1.0, 0) + half(1.0, 0), axis=-1),
                jnp.concatenate(half(-1.0, 1) + half(1.0, 1), axis=-1))

    a_half = AXIAL_DIM // 2
    a_row, a_col = _rope_cs(row, AXIAL_DIM, A_ROPE_THETA), _rope_cs(col, AXIAL_DIM, A_ROPE_THETA)
    tab_a = (place([(a_row, a_half), (None, a_half)], rows) + place([(None, a_half), (a_col, a_half)], GRID_W))
    p_half = PARTIAL_ROPE_DIM // 2
    rest = HEAD_DIM // 2 - p_half
    p_row = _rope_cs(row * GRID_W, PARTIAL_ROPE_DIM, PARTIAL_ROPE_THETA)
    p_col = _rope_cs(col, PARTIAL_ROPE_DIM, PARTIAL_ROPE_THETA)
    tab_b = (place([(p_row, p_half), (None, rest)], rows) + place([(p_col, p_half), (None, rest)], GRID_W))
    return tab_a, tab_b


def _tiles(m, seq, d_ff):
    def fit(n, t):
        t = min(t, n)
        while n % t:
            t //= 2
        return t
    return dict(
        ffn1_tm=fit(m, 1024), ffn2_tm=fit(m, 1024), ffn_tf=fit(d_ff, 512),
        proj_tm=fit(seq, 512), out_tm=fit(m, 512),
        a_tq=fit(seq, 256), b_tq=fit(seq, 512), a_tk=fit(seq, 4096), b_tk=fit(seq, 4096),
    )


def kernel(x, ffn1_norm, ffn1_w_gu, ffn1_w_down, mix_norm, w_in, a_q_norm, a_k_norm, b_q_norm, b_k_norm,
           b_lambda_q1, b_lambda_k1, b_lambda_q2, b_lambda_k2, b_subln, w_out, ffn2_norm, ffn2_w_gu,
           ffn2_w_down, out_norm):
    bsz, seq, d = x.shape
    depth = w_in.shape[0]
    d_ff = ffn1_w_down.shape[1]
    m = bsz * seq
    hd = HEAD_DIM
    a_heads = d // (2 * hd)
    a_kv = a_heads // A_GROUP
    b_vdim = 2 * hd
    b_heads = d // (2 * b_vdim)
    a_q, a_kvw = a_heads * hd, a_kv * hd
    b_qk, b_v = b_heads * 2 * hd, b_heads * b_vdim
    t = _tiles(m, seq, d_ff)
    q_scale = (hd ** -0.5) * LOG2E
    tab_a, tab_b = _rope_tables(seq)

    x = x.reshape(m, d)
    for l in range(depth):
        lambda_init = 0.8 - 0.6 * math.exp(-0.3 * l)
        row = lambda v: v[l].reshape(1, -1).astype(F32)
        w_gu1, w_d1 = ffn1_w_gu[l].astype(BF16), ffn1_w_down[l].astype(BF16)
        w_gu2, w_d2 = ffn2_w_gu[l].astype(BF16), ffn2_w_down[l].astype(BF16)
        w_o = w_out[l].astype(BF16)
        c1 = a_q + a_kvw
        c2 = c1 + a_kvw
        c3 = c2 + 2 * b_qk
        w_i = w_in[l]
        w_qk_a, w_v_a = _permute_heads(w_i[:, :c1], AXIAL_ORDER).astype(BF16), w_i[:, c1:c2].astype(BF16)
        w_qk_b, w_v_b = _permute_heads(w_i[:, c2:c3], PARTIAL_ORDER).astype(BF16), w_i[:, c3:].astype(BF16)

        x, h = _ffn(x, row(ffn1_norm), w_gu1, w_d1, row(mix_norm),
                    emit_norm=True, final_norm=False, tm=t["ffn1_tm"], tf=t["ffn_tf"])

        tqa, tqb, tka, tkb, ptm = t["a_tq"], t["b_tq"], t["a_tk"], t["b_tk"], t["proj_tm"]
        gain_a = jnp.concatenate([jnp.tile(_permute_heads(row(a_q_norm), AXIAL_ORDER) * q_scale, (1, a_heads)),
                                  jnp.tile(_permute_heads(row(a_k_norm), AXIAL_ORDER), (1, a_kv))], axis=-1)
        qa_t, ka, va_t = _proj(h, w_qk_a, w_v_a, gain_a, tab_a, bsz=bsz, seq=seq, n_qc=a_heads, n_kc=a_kv,
                               per_group=A_GROUP, dv=hd, tq=tqa, tk=tka, tm=ptm)
        gain_b = jnp.concatenate([jnp.tile(_permute_heads(row(b_q_norm), PARTIAL_ORDER) * q_scale, (1, 2 * b_heads)),
                                  jnp.tile(_permute_heads(row(b_k_norm), PARTIAL_ORDER), (1, 2 * b_heads))], axis=-1)
        qb_t, kb, vb_t = _proj(h, w_qk_b, w_v_b, gain_b, tab_b, bsz=bsz, seq=seq, n_qc=2 * b_heads, n_kc=2 * b_heads,
                               per_group=2, dv=b_vdim, tq=tqb, tk=tkb, tm=ptm)
        ka = ka.reshape(bsz, a_kv, 1, seq // tka, tka, hd)
        kb = kb.reshape(bsz, b_heads, 2, seq // tkb, tkb, hd)

        out_a = _attn(qa_t, ka, va_t, tq=tqa)
        out_b = _attn(qb_t, kb, vb_t, tq=tqb,
                      diff_args=(row(b_lambda_q1), row(b_lambda_k1), row(b_lambda_q2), row(b_lambda_k2), row(b_subln)),
                      lambda_init=lambda_init)

        x = _outproj(x, out_a.reshape(m, a_q), out_b.reshape(m, b_v), w_o, tm=t["out_tm"])
        x, = _ffn(x, row(ffn2_norm), w_gu2, w_d2, row(out_norm),
                  emit_norm=False, final_norm=True, tm=t["ffn2_tm"], tf=t["ffn_tf"])
    return x.reshape(bsz, seq, d)
```
